```python
import jax, jax.numpy as jnp
from jax import lax
import numpy as np

D_MODEL = 4096
BATCH = 4
SEQ = 4096
DEPTH = 4

CTX_LEN = 256
GRID_W = 64
HEAD_DIM = 128
MIX_WIDTH = D_MODEL

SWA_HEADS = MIX_WIDTH // 2 // HEAD_DIM
SWA_KV_HEADS = SWA_HEADS // 4
SWA_Q = SWA_HEADS * HEAD_DIM
SWA_KV = SWA_KV_HEADS * HEAD_DIM
SWA_WINDOW = 128
SWA_BLOCK = 128

CONV_W = MIX_WIDTH // 4
CONV_K = 3

NAT_HEADS = MIX_WIDTH // 4 // HEAD_DIM
NAT_W = NAT_HEADS * HEAD_DIM
NAT_ROWS_MAX = 8
NAT_COLS = 16
NAT_COL_BLOCK = 16
NAT_KEY_COLS = 32

OFF_SWA_K = SWA_Q
OFF_SWA_V = OFF_SWA_K + SWA_KV
OFF_CONV = OFF_SWA_V + SWA_KV
OFF_NAT_Q = OFF_CONV + 3 * CONV_W
OFF_NAT_K = OFF_NAT_Q + NAT_W
OFF_NAT_V = OFF_NAT_K + NAT_W
IN_COLS = OFF_NAT_V + NAT_W

ROPE_THETA = 10000.0
ROPE_FREQS = HEAD_DIM // 4

N_GROUPS = 4
EXPERTS_PER_GROUP = 4
N_EXPERTS = N_GROUPS * EXPERTS_PER_GROUP
TOP_K_EXPERT = 2
D_EXPERT = D_MODEL // 16

DEEPNORM_ALPHA = (2.0 * DEPTH) ** 0.25
DEEPNORM_BETA = (8.0 * DEPTH) ** -0.25
LN_EPS = 1e-5
RMS_EPS = 1e-6
ADA_INIT = 0.5
NEG_INF = -1e30

kernel_name = "hybrid_swa_conv_nat_hmoe_dit"


def layer_norm(x, g, b):
    xf = x.astype(jnp.float32)
    mu = jnp.mean(xf, axis=-1, keepdims=True)
    var = jnp.mean(jnp.square(xf - mu), axis=-1, keepdims=True)
    y = (xf - mu) * lax.rsqrt(var + LN_EPS)
    return (y * g.astype(jnp.float32) + b.astype(jnp.float32)).astype(x.dtype)


def rms_norm(x, g):
    xf = x.astype(jnp.float32)
    y = xf * lax.rsqrt(jnp.mean(jnp.square(xf), axis=-1, keepdims=True) + RMS_EPS)
    return (y * g.astype(jnp.float32)).astype(x.dtype)


def modulate(h, shift, scale):
    return h * (1 + scale) + shift


def split_heads(t, n_heads):
    return t.reshape(t.shape[:-1] + (n_heads, HEAD_DIM))


def joint_softmax(*logits):
    z = jnp.concatenate([l.astype(jnp.float32) for l in logits], axis=-1)
    return jax.nn.softmax(z, axis=-1)


def axial_rope_tables(n_tokens):
    t = jnp.arange(n_tokens, dtype=jnp.int32)
    pos = jnp.stack([t // GRID_W, t % GRID_W], axis=-1).astype(jnp.float32)
    inv_freq = ROPE_THETA ** (-jnp.arange(ROPE_FREQS, dtype=jnp.float32) / ROPE_FREQS)
    ang = pos[:, :, None] * inv_freq
    return jnp.cos(ang), jnp.sin(ang)


def apply_axial_rope(x, cos, sin):
    xs = x.reshape(x.shape[:-1] + (2, 2, ROPE_FREQS))
    x1, x2 = xs[..., 0, :], xs[..., 1, :]
    cs = cos[:, None].astype(x.dtype)
    sn = sin[:, None].astype(x.dtype)
    out = jnp.stack([x1 * cs - x2 * sn, x2 * cs + x1 * sn], axis=-2)
    return out.reshape(x.shape)


def context_attention(q, k, v, sink):
    Bn, L, HQ, _ = q.shape
    HKV = k.shape[2]
    G = HQ // HKV
    qg = (q * HEAD_DIM ** -0.5).reshape(Bn, L, HKV, G, HEAD_DIM)
    s = jnp.einsum('bqhgd,bkhd->bhgqk', qg, k)
    parts = [s]
    if sink is not None:
        parts.append(jnp.broadcast_to(sink.reshape(HKV, G, 1, 1), s.shape[:-1] + (1,)))
    p = joint_softmax(*parts)[..., :L].astype(v.dtype)
    o = jnp.einsum('bhgqk,bkhd->bqhgd', p, v)
    return o.reshape(Bn, L, HQ * HEAD_DIM)


def window_attention(q, k, v, kc, vc, sink):
    Bn, S, HQ, _ = q.shape
    HKV = k.shape[2]
    G = HQ // HKV
    nb = S // SWA_BLOCK
    qb = (q * HEAD_DIM ** -0.5).reshape(Bn, nb, SWA_BLOCK, HKV, G, HEAD_DIM)
    pad = ((0, 0), (SWA_BLOCK, SWA_BLOCK), (0, 0), (0, 0))
    kp = jnp.pad(k, pad).reshape(Bn, nb + 2, SWA_BLOCK, HKV, HEAD_DIM)
    vp = jnp.pad(v, pad).reshape(Bn, nb + 2, SWA_BLOCK, HKV, HEAD_DIM)
    kband = jnp.concatenate([kp[:, :-2], kp[:, 1:-1], kp[:, 2:]], axis=2)
    vband = jnp.concatenate([vp[:, :-2], vp[:, 1:-1], vp[:, 2:]], axis=2)
    s_win = jnp.einsum('bnqhgd,bnkhd->bnhgqk', qb, kband).astype(jnp.float32)
    qi = jnp.arange(SWA_BLOCK)[:, None]
    kj = jnp.arange(3 * SWA_BLOCK)[None, :]
    in_band = jnp.abs(kj - SWA_BLOCK - qi) <= SWA_WINDOW
    kpos = (jnp.arange(nb)[:, None, None] - 1) * SWA_BLOCK + kj[None]
    valid = in_band[None] & (kpos >= 0) & (kpos < S)
    s_win = jnp.where(valid[None, :, None, None], s_win, NEG_INF)
    s_ctx = jnp.einsum('bnqhgd,bkhd->bnhgqk', qb, kc)
    s_sink = jnp.broadcast_to(sink.reshape(1, 1, HKV, G, 1, 1), s_ctx.shape[:-1] + (1,))
    p = joint_softmax(s_win, s_ctx, s_sink).astype(v.dtype)
    nk = 3 * SWA_BLOCK
    L = kc.shape[1]
    o = (jnp.einsum('bnhgqk,bnkhd->bnqhgd', p[..., :nk], vband)
         + jnp.einsum('bnhgqk,bkhd->bnqhgd', p[..., nk:nk + L], vc))
    return o.reshape(Bn, S, HQ * HEAD_DIM)


def neighbourhood_attention(q, k, v, kc, vc, rpb):
    Bn, S, H, _ = q.shape
    rows = S // GRID_W
    wr = min(NAT_ROWS_MAX, rows)
    ncb = GRID_W // NAT_COL_BLOCK
    nk = wr * NAT_KEY_COLS
    kg = k.reshape(Bn, rows, GRID_W, H, HEAD_DIM)
    vg = v.reshape(Bn, rows, GRID_W, H, HEAD_DIM)
    qg = jnp.moveaxis((q * HEAD_DIM ** -0.5).reshape(Bn, rows, ncb, NAT_COL_BLOCK, H, HEAD_DIM), 1, 0)
    qcol = jnp.arange(GRID_W).reshape(ncb, NAT_COL_BLOCK)
    col_start = jnp.clip(qcol - NAT_COLS // 2, 0, GRID_W - NAT_COLS)
    strip_start = jnp.clip(jnp.arange(ncb) * NAT_COL_BLOCK - NAT_COLS // 2, 0, GRID_W - NAT_KEY_COLS)
    kcol = strip_start[:, None] + jnp.arange(NAT_KEY_COLS)[None, :]
    col_ok = ((kcol[:, None, :] >= col_start[..., None])
              & (kcol[:, None, :] < col_start[..., None] + NAT_COLS))
    mask = jnp.broadcast_to(col_ok[:, :, None, :], (ncb, NAT_COL_BLOCK, wr, NAT_KEY_COLS)).reshape(ncb, NAT_COL_BLOCK, nk)
    dc_idx = jnp.clip(kcol[:, None, :] - qcol[..., None] + NAT_COLS - 1, 0, 2 * NAT_COLS - 2)

    def row_block(args):
        r, q_r = args
        rs = jnp.clip(r - wr // 2, 0, rows - wr)
        k_rows = lax.dynamic_slice_in_dim(kg, rs, wr, axis=1)
        v_rows = lax.dynamic_slice_in_dim(vg, rs, wr, axis=1)
        k_blk = jnp.transpose(k_rows[:, :, kcol], (0, 2, 1, 3, 4, 5)).reshape(Bn, ncb, nk, H, HEAD_DIM)
        v_blk = jnp.transpose(v_rows[:, :, kcol], (0, 2, 1, 3, 4, 5)).reshape(Bn, ncb, nk, H, HEAD_DIM)
        dr_idx = rs + jnp.arange(wr) - r + NAT_ROWS_MAX - 1
        bias = rpb[:, dr_idx[:, None, None, None], dc_idx[None]]
        bias = jnp.transpose(bias, (2, 0, 3, 1, 4)).reshape(ncb, H, NAT_COL_BLOCK, nk)
        s = jnp.einsum('bnqhd,bnkhd->bnhqk', q_r, k_blk).astype(jnp.float32) + bias[None].astype(jnp.float32)
        s = jnp.where(mask[None, :, None], s, NEG_INF)
        s_ctx = jnp.einsum('bnqhd,bkhd->bnhqk', q_r, kc)
        p = joint_softmax(s, s_ctx).astype(v.dtype)
        return (jnp.einsum('bnhqk,bnkhd->bnqhd', p[..., :nk], v_blk)
                + jnp.einsum('bnhqk,bkhd->bnqhd', p[..., nk:], vc))

    out = lax.map(row_block, (jnp.arange(rows, dtype=jnp.int32), qg))
    return jnp.moveaxis(out, 0, 1).reshape(Bn, S, H * HEAD_DIM)


def short_conv(z, w, b):
    T = z.shape[1]
    pad = CONV_K // 2
    zp = jnp.pad(z, ((0, 0), (pad, pad), (0, 0)))
    out = b
    for j in range(CONV_K):
        out = out + zp[:, j:j + T] * w[j]
    return out


def gated_short_conv(u3, w, b):
    u, b_gate, c_gate = jnp.split(u3, 3, axis=-1)
    return b_gate * short_conv(c_gate * u, w, b)


def mix_output(y_swa, y_conv, y_nat, g, w_out):
    y = jnp.concatenate([
        rms_norm(y_swa, g[:SWA_Q]),
        rms_norm(y_conv, g[SWA_Q:SWA_Q + CONV_W]),
        rms_norm(y_nat, g[SWA_Q + CONV_W:]),
    ], axis=-1)
    return y @ w_out


def hierarchical_moe(h, w_rg, b_rg, w_re, b_re, w_gate, w_up, w_down):
    lead = h.shape[:-1]
    D = h.shape[-1]
    t = h.reshape(-1, D)
    N = t.shape[0]
    g_prob = jax.nn.softmax((t @ w_rg).astype(jnp.float32) + b_rg.astype(jnp.float32), axis=-1)
    g_p, g_idx = lax.top_k(g_prob, 1)
    onehot_g = jax.nn.one_hot(g_idx[:, 0], N_GROUPS, dtype=jnp.float32)
    e_logits = ((t @ w_re).astype(jnp.float32) + b_re.astype(jnp.float32)).reshape(N, N_GROUPS, EXPERTS_PER_GROUP)
    e_in = jnp.einsum('ng,nge->ne', onehot_g, e_logits)
    e_val, e_idx = lax.top_k(e_in, TOP_K_EXPERT)
    e_w = jax.nn.softmax(e_val, axis=-1) * g_p
    within = jnp.sum(jax.nn.one_hot(e_idx, EXPERTS_PER_GROUP, dtype=jnp.float32) * e_w[..., None], axis=1)
    gates = (onehot_g[:, :, None] * within[:, None, :]).reshape(N, N_EXPERTS).astype(t.dtype)
    hid = jax.nn.silu(jnp.einsum('nd,edf->nef', t, w_gate)) * jnp.einsum('nd,edf->nef', t, w_up)
    out = jnp.einsum('nef,efd->nd', hid * gates[..., None], w_down)
    return out.reshape(lead + (D,))


def setup_inputs(seed: int = 0) -> dict:
    key = jax.random.key(seed)
    ks = jax.random.split(key, 24)
    f32 = jnp.float32
    nrm = jax.random.normal
    D = D_MODEL
    return {
        "x": nrm(ks[0], (BATCH, SEQ, D), f32),
        "c": nrm(ks[1], (BATCH, D), f32),
        "ctx": nrm(ks[2], (BATCH, CTX_LEN, D), f32),
        "c_ctx": nrm(ks[3], (D,), f32),
        "w_ada": nrm(ks[4], (DEPTH, D, 6 * D), f32) * (ADA_INIT * D ** -0.5),
        "b_ada": nrm(ks[5], (DEPTH, 6 * D), f32) * 0.01,
        "w_in": nrm(ks[6], (DEPTH, D, IN_COLS), f32) * D ** -0.5,
        "conv_w": nrm(ks[7], (DEPTH, CONV_K, CONV_W), f32) * CONV_K ** -0.5,
        "conv_b": nrm(ks[8], (DEPTH, CONV_W), f32) * 0.01,
        "attn_sink": nrm(ks[9], (DEPTH, SWA_HEADS), f32) * 0.5,
        "nat_rpb": nrm(ks[10], (DEPTH, NAT_HEADS, 2 * NAT_ROWS_MAX - 1, 2 * NAT_COLS - 1), f32) * 0.1,
        "mix_norm_g": 1.0 + 0.02 * nrm(ks[11], (DEPTH, MIX_WIDTH), f32),
        "w_out": nrm(ks[12], (DEPTH, MIX_WIDTH, D), f32) * (MIX_WIDTH ** -0.5 * DEEPNORM_BETA),
        "ln1_g": 1.0 + 0.02 * nrm(ks[13], (DEPTH, D), f32),
        "ln1_b": 0.02 * nrm(ks[14], (DEPTH, D), f32),
        "w_router_group": nrm(ks[15], (DEPTH, D, N_GROUPS), f32) * D ** -0.5,
        "b_router_group": nrm(ks[16], (DEPTH, N_GROUPS), f32) * 0.01,
        "w_router_expert": nrm(ks[17], (DEPTH, D, N_EXPERTS), f32) * D ** -0.5,
        "b_router_expert": nrm(ks[18], (DEPTH, N_EXPERTS), f32) * 0.01,
        "w_gate": nrm(ks[19], (DEPTH, N_EXPERTS, D, D_EXPERT), f32) * D ** -0.5,
        "w_up": nrm(ks[20], (DEPTH, N_EXPERTS, D, D_EXPERT), f32) * D ** -0.5,
        "w_down": nrm(ks[21], (DEPTH, N_EXPERTS, D_EXPERT, D), f32) * (D_EXPERT ** -0.5 * DEEPNORM_BETA),
        "ln2_g": 1.0 + 0.02 * nrm(ks[22], (DEPTH, D), f32),
        "ln2_b": 0.02 * nrm(ks[23], (DEPTH, D), f32),
    }


def reference(x, c, ctx, c_ctx, w_ada, b_ada, w_in, conv_w, conv_b, attn_sink, nat_rpb, mix_norm_g, w_out,
              ln1_g, ln1_b, w_router_group, b_router_group, w_router_expert, b_router_expert,
              w_gate, w_up, w_down, ln2_g, ln2_b):
    D = x.shape[-1]
    cos, sin = axial_rope_tables(x.shape[1])
    act_lat = jax.nn.silu(c)
    act_ctx = jax.nn.silu(c_ctx)
    h, hc = x, ctx
    for i in range(DEPTH):
        last = i == DEPTH - 1
        w_in_i = w_in[i]
        moe_params = (w_router_group[i], b_router_group[i], w_router_expert[i], b_router_expert[i],
                      w_gate[i], w_up[i], w_down[i])
        mod = act_lat @ w_ada[i] + b_ada[i]
        sh_m, sc_m, g_m, sh_f, sc_f, g_f = jnp.split(mod[:, None, :], 6, axis=-1)
        n_mod_c = 2 * D if last else 6 * D
        mod_c = (act_ctx @ w_ada[i][:, :n_mod_c] + b_ada[i][:n_mod_c]).reshape(-1, D)
        ctx_in = modulate(hc, mod_c[0], mod_c[1])

        kv_swa_c = ctx_in @ w_in_i[:, OFF_SWA_K:OFF_CONV]
        kv_nat_c = ctx_in @ w_in_i[:, OFF_NAT_K:IN_COLS]
        k_swa_c = split_heads(kv_swa_c[..., :SWA_KV], SWA_KV_HEADS)
        v_swa_c = split_heads(kv_swa_c[..., SWA_KV:], SWA_KV_HEADS)
        k_nat_c = split_heads(kv_nat_c[..., :NAT_W], NAT_HEADS)
        v_nat_c = split_heads(kv_nat_c[..., NAT_W:], NAT_HEADS)

        u = modulate(h, sh_m, sc_m) @ w_in_i
        q_swa = apply_axial_rope(split_heads(u[..., :OFF_SWA_K], SWA_HEADS), cos, sin)
        k_swa = apply_axial_rope(split_heads(u[..., OFF_SWA_K:OFF_SWA_V], SWA_KV_HEADS), cos, sin)
        v_swa = split_heads(u[..., OFF_SWA_V:OFF_CONV], SWA_KV_HEADS)
        y_swa = window_attention(q_swa, k_swa, v_swa, k_swa_c, v_swa_c, attn_sink[i])
        y_conv = gated_short_conv(u[..., OFF_CONV:OFF_NAT_Q], conv_w[i], conv_b[i])
        y_nat = neighbourhood_attention(split_heads(u[..., OFF_NAT_Q:OFF_NAT_K], NAT_HEADS),
                                        split_heads(u[..., OFF_NAT_K:OFF_NAT_V], NAT_HEADS),
                                        split_heads(u[..., OFF_NAT_V:IN_COLS], NAT_HEADS),
                                        k_nat_c, v_nat_c, nat_rpb[i])
        mix = mix_output(y_swa, y_conv, y_nat, mix_norm_g[i], w_out[i])
        h_new = layer_norm(DEEPNORM_ALPHA * h + g_m * mix, ln1_g[i], ln1_b[i])
        ffn = hierarchical_moe(modulate(h_new, sh_f, sc_f), *moe_params)
        h_new = layer_norm(DEEPNORM_ALPHA * h_new + g_f * ffn, ln2_g[i], ln2_b[i])

        if not last:
            q_swa_c = split_heads(ctx_in @ w_in_i[:, :OFF_SWA_K], SWA_HEADS)
            conv_c = ctx_in @ w_in_i[:, OFF_CONV:OFF_NAT_Q]
            q_nat_c = split_heads(ctx_in @ w_in_i[:, OFF_NAT_Q:OFF_NAT_K], NAT_HEADS)
            yc_swa = context_attention(q_swa_c, k_swa_c, v_swa_c, attn_sink[i])
            yc_conv = gated_short_conv(conv_c, conv_w[i], conv_b[i])
            yc_nat = context_attention(q_nat_c, k_nat_c, v_nat_c, None)
            mix_c = mix_output(yc_swa, yc_conv, yc_nat, mix_norm_g[i], w_out[i])
            hc_new = layer_norm(DEEPNORM_ALPHA * hc + mod_c[2] * mix_c, ln1_g[i], ln1_b[i])
            ffn_c = hierarchical_moe(modulate(hc_new, mod_c[3], mod_c[4]), *moe_params)
            hc = layer_norm(DEEPNORM_ALPHA * hc_new + mod_c[5] * ffn_c, ln2_g[i], ln2_b[i])
        h = h_new
    return h
```

```python
import functools

import numpy as np
import jax
import jax.numpy as jnp
from jax import lax
from jax.experimental import pallas as pl
from jax.experimental.pallas import tpu as pltpu

F32 = jnp.float32
BF16 = jnp.bfloat16

HEAD_DIM = 128
GRID_W = 64
ROPE_THETA = 10000.0
ROPE_FREQS = HEAD_DIM // 4
SWA_GROUP = 4
SWA_BLOCK = 128
NAT_ROWS = 8
NAT_COLS = 16
NAT_QROWS = 4
NAT_QBLOCK = NAT_QROWS * GRID_W
N_GROUPS = 4
EXPERTS_PER_GROUP = 4
N_EXPERTS = N_GROUPS * EXPERTS_PER_GROUP
ROUTER_LANES = 128
LN_EPS = 1e-5
RMS_EPS = 1e-6
NEG_INF = -1e30
ATTN_SCALE = HEAD_DIM ** -0.5
MOD_ROWS = 8
VMEM_LIMIT = 56 * 1024 * 1024


def _params(n_axes):
    return pltpu.CompilerParams(dimension_semantics=("arbitrary",) * n_axes,
                                vmem_limit_bytes=VMEM_LIMIT)


class Dims:
    def __init__(self, B, S, L, D, depth, d_expert):
        self.B, self.S, self.L, self.D, self.depth, self.d_expert = B, S, L, D, depth, d_expert
        self.swa_q = D // 2
        self.swa_heads = self.swa_q // HEAD_DIM
        self.swa_kv_heads = self.swa_heads // SWA_GROUP
        self.swa_kv = self.swa_kv_heads * HEAD_DIM
        self.conv_w = D // 4
        self.nat_w = D // 4
        self.nat_heads = self.nat_w // HEAD_DIM
        self.off_swa_k = self.swa_q
        self.off_swa_v = self.off_swa_k + self.swa_kv
        self.off_conv = self.off_swa_v + self.swa_kv
        self.off_nat_q = self.off_conv + 3 * self.conv_w
        self.off_nat_k = self.off_nat_q + self.nat_w
        self.off_nat_v = self.off_nat_k + self.nat_w
        self.in_cols = self.off_nat_v + self.nat_w
        self.n_lat = B * S
        self.n_ctx = B * L
        self.rows = self.n_lat + self.n_ctx
        self.grid_rows = S // GRID_W
        self.alpha = (2.0 * depth) ** 0.25
        self.tm = 1024 if (S % 1024 == 0 and self.n_ctx % 1024 == 0) else 256
        self.tr = 256
        self.tn = self.swa_kv
        assert S % self.tm == 0 and self.n_ctx % self.tm == 0
        assert S % NAT_QBLOCK == 0 and L % NAT_QBLOCK == 0 and self.grid_rows >= 3 * NAT_QROWS
        assert S // SWA_BLOCK >= 3 and L % SWA_BLOCK == 0
        assert self.off_conv % (3 * self.conv_w) == 0
        assert B + 1 <= MOD_ROWS


def _ada_kernel(x_ref, w_ref, b_ref, o_ref):
    x = x_ref[...]
    act = x * jax.nn.sigmoid(x)
    a_hi = act.astype(BF16)
    a_lo = (act - a_hi.astype(F32)).astype(BF16)
    w = w_ref[0]
    w_hi = w.astype(BF16)
    w_lo = (w - w_hi.astype(F32)).astype(BF16)
    acc = jnp.dot(a_hi, w_hi, preferred_element_type=F32)
    acc += jnp.dot(a_lo, w_hi, preferred_element_type=F32)
    acc += jnp.dot(a_hi, w_lo, preferred_element_type=F32)
    o_ref[0] = acc + b_ref[0]


def _ada_all(xin, w_ada, b_ada):
    depth, D, n6 = w_ada.shape
    tn = 512
    return pl.pallas_call(
        _ada_kernel,
        grid=(depth, n6 // tn),
        in_specs=[pl.BlockSpec((MOD_ROWS, D), lambda l, j: (0, 0)),
                  pl.BlockSpec((1, D, tn), lambda l, j: (l, 0, j)),
                  pl.BlockSpec((1, 1, tn), lambda l, j: (l, 0, j))],
        out_specs=pl.BlockSpec((1, MOD_ROWS, tn), lambda l, j: (l, 0, j)),
        out_shape=jax.ShapeDtypeStruct((depth, MOD_ROWS, n6), F32),
        compiler_params=_params(2),
        name="ada_mod",
    )(xin, w_ada, b_ada.reshape(depth, 1, n6))


def _mod_spec(dm, tile):
    n6 = 6 * dm.D
    return pl.BlockSpec((1, 1, n6), lambda i, *_: (jnp.minimum(i * tile // dm.S, dm.B), 0, 0))


def _modulate_kernel(h_ref, mod_ref, o_ref, *, D):
    mod = mod_ref[0]
    o_ref[...] = (h_ref[...] * (1.0 + mod[:, D:2 * D]) + mod[:, 0:D]).astype(BF16)


def _modulate(dm, h, mod):
    tr = dm.tr
    return pl.pallas_call(
        functools.partial(_modulate_kernel, D=dm.D),
        grid=(dm.rows // tr,),
        in_specs=[pl.BlockSpec((tr, dm.D), lambda i: (i, 0)), _mod_spec(dm, tr)],
        out_specs=pl.BlockSpec((tr, dm.D), lambda i: (i, 0)),
        out_shape=jax.ShapeDtypeStruct((dm.rows, dm.D), BF16),
        compiler_params=_params(1),
        name="modulate_in",
    )(h, mod)


def _rope_tables(dm):
    t = np.arange(dm.S)
    pos = np.stack([t // GRID_W, t % GRID_W], axis=-1).astype(np.float32)
    inv_freq = jnp.asarray(ROPE_THETA, F32) ** (-jnp.arange(ROPE_FREQS, dtype=F32) / ROPE_FREQS)
    ang = jnp.asarray(pos)[:, :, None] * inv_freq
    cos, sin = jnp.cos(ang), jnp.sin(ang)
    cos_t = jnp.concatenate([cos, cos], axis=-1).reshape(dm.S, HEAD_DIM)
    sin_t = jnp.concatenate([-sin, sin], axis=-1).reshape(dm.S, HEAD_DIM)
    cos_t = jnp.concatenate([cos_t, jnp.ones((dm.tm, HEAD_DIM), F32)], axis=0)
    sin_t = jnp.concatenate([sin_t, jnp.zeros((dm.tm, HEAD_DIM), F32)], axis=0)
    return cos_t, sin_t


def _inproj_kernel(a_ref, w_ref, cos_ref, sin_ref, o_ref, *, tn, n_rope_q, n_rope, nat_q0, nat_q1):
    j = pl.program_id(1)
    acc = jnp.dot(a_ref[...], w_ref[...], preferred_element_type=F32)
    is_rope = j < n_rope
    is_natq = jnp.logical_and(j >= nat_q0, j < nat_q1)

    @pl.when(is_rope)
    def _():
        cos = cos_ref[...]
        sin = sin_ref[...]
        scale = jnp.where(j < n_rope_q, ATTN_SCALE, 1.0).astype(F32)
        lane = lax.broadcasted_iota(jnp.int32, cos.shape, 1)
        first_half = (lane % (2 * ROPE_FREQS)) < ROPE_FREQS
        for c in range(tn // HEAD_DIM):
            x = acc[:, c * HEAD_DIM:(c + 1) * HEAD_DIM]
            partner = jnp.where(first_half,
                                pltpu.roll(x, HEAD_DIM - ROPE_FREQS, 1),
                                pltpu.roll(x, ROPE_FREQS, 1))
            o_ref[:, c * HEAD_DIM:(c + 1) * HEAD_DIM] = ((x * cos + partner * sin) * scale).astype(BF16)

    @pl.when(is_natq)
    def _():
        o_ref[...] = (acc * ATTN_SCALE).astype(BF16)

    @pl.when(jnp.logical_not(jnp.logical_or(is_rope, is_natq)))
    def _():
        o_ref[...] = acc.astype(BF16)


def _inproj(dm, a, w, cos_t, sin_t):
    tm, tn = dm.tm, dm.tn
    lat_tiles = dm.n_lat // tm
    per_seq = dm.S // tm

    def tab_map(i, j):
        return (jnp.where(i < lat_tiles, i % per_seq, per_seq), 0)

    kern = functools.partial(_inproj_kernel, tn=tn, n_rope_q=dm.swa_q // tn, n_rope=dm.off_swa_v // tn,
                             nat_q0=dm.off_nat_q // tn, nat_q1=dm.off_nat_k // tn)
    return pl.pallas_call(
        kern,
        grid=(dm.rows // tm, dm.in_cols // tn),
        in_specs=[pl.BlockSpec((tm, dm.D), lambda i, j: (i, 0)),
                  pl.BlockSpec((dm.D, tn), lambda i, j: (0, j)),
                  pl.BlockSpec((tm, HEAD_DIM), tab_map),
                  pl.BlockSpec((tm, HEAD_DIM), tab_map)],
        out_specs=pl.BlockSpec((tm, tn), lambda i, j: (i, j)),
        out_shape=jax.ShapeDtypeStruct((dm.rows, dm.in_cols), BF16),
        compiler_params=_params(2),
        name="in_proj",
    )(a, w, cos_t, sin_t)


def _mm_kernel(*refs, k_splits):
    a_refs, w_ref, o_ref = refs[:-2], refs[-2], refs[-1]
    acc = None
    k0 = 0
    for a_ref, kw in zip(a_refs, k_splits):
        part = jnp.dot(a_ref[...], w_ref[k0:k0 + kw, :], preferred_element_type=F32)
        acc = part if acc is None else acc + part
        k0 += kw
    o_ref[...] = acc.astype(o_ref.dtype)


def _matmul(a_list, w, rows, tm, tn, out_dtype, name):
    K, N = w.shape
    k_splits = tuple(a.shape[1] for a in a_list)
    assert sum(k_splits) == K
    in_specs = [pl.BlockSpec((tm, kw), lambda i, j: (i, 0)) for kw in k_splits]
    in_specs.append(pl.BlockSpec((K, tn), lambda i, j: (0, j)))
    return pl.pallas_call(
        functools.partial(_mm_kernel, k_splits=k_splits),
        grid=(rows // tm, N // tn),
        in_specs=in_specs,
        out_specs=pl.BlockSpec((tm, tn), lambda i, j: (i, j)),
        out_shape=jax.ShapeDtypeStruct((rows, N), out_dtype),
        compiler_params=_params(2),
        name=name,
    )(*a_list, w)


def _dot_nt(a, b):
    return lax.dot_general(a, b, (((1,), (1,)), ((), ())), preferred_element_type=F32)


def _rms_gain(o, g):
    ms = jnp.mean(o * o, axis=-1, keepdims=True)
    return (o * lax.rsqrt(ms + RMS_EPS) * g).astype(BF16)


def _swa_mask_table():
    qi = np.arange(SWA_BLOCK)[:, None]
    kj = np.arange(SWA_BLOCK)[None, :]
    prev_ok = kj >= qi
    cur_ok = np.ones((SWA_BLOCK, SWA_BLOCK), bool)
    next_ok = kj <= qi
    none = np.zeros((SWA_BLOCK, SWA_BLOCK), bool)
    kinds = [np.concatenate([none, cur_ok, next_ok], 1),
             np.concatenate([prev_ok, cur_ok, next_ok], 1),
             np.concatenate([prev_ok, cur_ok, none], 1),
             np.concatenate([none, none, none], 1)]
    return jnp.asarray(np.where(np.stack(kinds), 0.0, NEG_INF).astype(np.float32))


def _swa_kernel(sink_ref, q_ref, kp_ref, kc_ref, kn_ref, vp_ref, vc_ref, vn_ref, kx_ref, vx_ref,
                mask_ref, g_ref, o_ref, acc_ref, *, kv_heads):
    mask = mask_ref[0]
    for hk in range(kv_heads):
        ks = slice(hk * HEAD_DIM, (hk + 1) * HEAD_DIM)
        q = jnp.concatenate([q_ref[:, (hk * SWA_GROUP + g) * HEAD_DIM:(hk * SWA_GROUP + g + 1) * HEAD_DIM]
                             for g in range(SWA_GROUP)], axis=0)
        kb = jnp.concatenate([kp_ref[:, ks], kc_ref[:, ks], kn_ref[:, ks]], axis=0)
        vb = jnp.concatenate([vp_ref[:, ks], vc_ref[:, ks], vn_ref[:, ks]], axis=0)
        s = _dot_nt(q, kb)
        s = (s.reshape(SWA_GROUP, SWA_BLOCK, 3 * SWA_BLOCK) + mask[None]).reshape(SWA_GROUP * SWA_BLOCK, 3 * SWA_BLOCK)
        sx = _dot_nt(q, kx_ref[:, ks])
        sink = jnp.concatenate([jnp.full((SWA_BLOCK, 1), sink_ref[hk * SWA_GROUP + g], F32)
                                for g in range(SWA_GROUP)], axis=0)
        m = jnp.maximum(jnp.maximum(jnp.max(s, axis=-1, keepdims=True), jnp.max(sx, axis=-1, keepdims=True)), sink)
        p = jnp.exp(s - m)
        px = jnp.exp(sx - m)
        denom = jnp.sum(p, axis=-1, keepdims=True) + jnp.sum(px, axis=-1, keepdims=True) + jnp.exp(sink - m)
        o = jnp.dot(p.astype(BF16), vb, preferred_element_type=F32)
        o += jnp.dot(px.astype(BF16), vx_ref[:, ks], preferred_element_type=F32)
        o = o / denom
        for g in range(SWA_GROUP):
            h = hk * SWA_GROUP + g
            acc_ref[:, h * HEAD_DIM:(h + 1) * HEAD_DIM] = o[g * SWA_BLOCK:(g + 1) * SWA_BLOCK]
    o_ref[...] = _rms_gain(acc_ref[...], g_ref[...])


def _swa(dm, u, sink, mask_tab, gain):
    nb = dm.S // SWA_BLOCK
    lat_blocks = dm.B * nb
    ctx_per = dm.L // SWA_BLOCK
    n_blocks = lat_blocks + dm.B * ctx_per
    kvw = dm.swa_kv
    k_col = dm.off_swa_k // kvw
    v_col = dm.off_swa_v // kvw
    ctx_row0 = dm.n_lat // dm.L

    def batch_of(g):
        return jnp.where(g < lat_blocks, g // nb, (g - lat_blocks) // ctx_per)

    def nbr(g, d):
        n = g % nb
        return jnp.where(g < lat_blocks, (g // nb) * nb + jnp.clip(n + d, 0, nb - 1), g)

    def kind(g):
        n = g % nb
        return jnp.where(g < lat_blocks, jnp.where(n == 0, 0, jnp.where(n == nb - 1, 2, 1)), 3)

    blk = lambda col, d: pl.BlockSpec((SWA_BLOCK, kvw), lambda g: (nbr(g, d), col))
    ctx = lambda col: pl.BlockSpec((dm.L, kvw), lambda g: (ctx_row0 + batch_of(g), col))
    in_specs = [pl.BlockSpec(memory_space=pltpu.SMEM),
                pl.BlockSpec((SWA_BLOCK, dm.swa_q), lambda g: (g, 0)),
                blk(k_col, -1), blk(k_col, 0), blk(k_col, 1),
                blk(v_col, -1), blk(v_col, 0), blk(v_col, 1),
                ctx(k_col), ctx(v_col),
                pl.BlockSpec((1, SWA_BLOCK, 3 * SWA_BLOCK), lambda g: (kind(g), 0, 0)),
                pl.BlockSpec((1, dm.swa_q), lambda g: (0, 0))]
    return pl.pallas_call(
        functools.partial(_swa_kernel, kv_heads=dm.swa_kv_heads),
        grid=(n_blocks,),
        in_specs=in_specs,
        out_specs=pl.BlockSpec((SWA_BLOCK, dm.swa_q), lambda g: (g, 0)),
        out_shape=jax.ShapeDtypeStruct((dm.rows, dm.swa_q), BF16),
        scratch_shapes=[pltpu.VMEM((SWA_BLOCK, dm.swa_q), F32)],
        compiler_params=_params(1),
        name="window_attn",
    )(sink, u, u, u, u, u, u, u, u, u, mask_tab, gain)


def _nat_bias_tables(dm, rpb_all):
    nq, nk = NAT_QROWS, 3 * NAT_QROWS
    i = np.arange(nq)[:, None]
    j = np.arange(nk)[None, :]
    row_ok = [(j >= nq) & (j < nq + NAT_ROWS) & (i >= 0),
              (j - i >= 0) & (j - i < NAT_ROWS),
              (j >= 0) & (j < NAT_ROWS) & (i >= 0)]
    dr = np.clip(j - i + (NAT_ROWS - 1) - nq, 0, 2 * NAT_ROWS - 2)
    t_row = np.zeros((3, 2 * NAT_ROWS - 1, nq, nk), np.float32)
    for k in range(3):
        for a in range(nq):
            for b in range(nk):
                if row_ok[k][a, b]:
                    t_row[k, dr[a, b], a, b] = 1.0
    cq = np.arange(GRID_W)[:, None]
    kc = np.arange(GRID_W)[None, :]
    cs = np.clip(cq - NAT_COLS // 2, 0, GRID_W - NAT_COLS)
    col_ok = (kc >= cs) & (kc < cs + NAT_COLS)
    dc = np.clip(kc - cq + NAT_COLS - 1, 0, 2 * NAT_COLS - 2)
    t_col = np.zeros((2 * NAT_COLS - 1, GRID_W, GRID_W), np.float32)
    for a in range(GRID_W):
        for b in range(GRID_W):
            if col_ok[a, b]:
                t_col[dc[a, b], a, b] = 1.0
    t_row, t_col = jnp.asarray(t_row), jnp.asarray(t_col)
    val = jnp.einsum('lhrc,trij,cqk->lthiqjk', rpb_all, t_row, t_col, precision=lax.Precision.HIGHEST)
    ok = jnp.einsum('trij,cqk->tiqjk', t_row, t_col) > 0.5
    bias = jnp.where(ok[None, :, None], val, NEG_INF)
    depth, _, H = bias.shape[:3]
    bias = bias.reshape(depth, 3, H, NAT_QBLOCK, 3 * NAT_QBLOCK)
    none = jnp.full((depth, 1, H, NAT_QBLOCK, 3 * NAT_QBLOCK), NEG_INF, F32)
    return jnp.concatenate([bias, none], axis=1)


def _nat_kernel(q_ref, kp_ref, kc_ref, kn_ref, vp_ref, vc_ref, vn_ref, kx_ref, vx_ref,
                bias_ref, g_ref, o_ref, acc_ref, *, heads):
    for h in range(heads):
        hs = slice(h * HEAD_DIM, (h + 1) * HEAD_DIM)
        q = q_ref[:, hs]
        kb = jnp.concatenate([kp_ref[:, hs], kc_ref[:, hs], kn_ref[:, hs]], axis=0)
        vb = jnp.concatenate([vp_ref[:, hs], vc_ref[:, hs], vn_ref[:, hs]], axis=0)
        s = _dot_nt(q, kb) + bias_ref[0, h]
        sx = _dot_nt(q, kx_ref[:, hs])
        m = jnp.maximum(jnp.max(s, axis=-1, keepdims=True), jnp.max(sx, axis=-1, keepdims=True))
        p = jnp.exp(s - m)
        px = jnp.exp(sx - m)
        denom = jnp.sum(p, axis=-1, keepdims=True) + jnp.sum(px, axis=-1, keepdims=True)
        o = jnp.dot(p.astype(BF16), vb, preferred_element_type=F32)
        o += jnp.dot(px.astype(BF16), vx_ref[:, hs], preferred_element_type=F32)
        acc_ref[:, hs] = o / denom
    o_ref[...] = _rms_gain(acc_ref[...], g_ref[...])


def _nat(dm, u, bias_tab, gain):
    nb = dm.S // NAT_QBLOCK
    lat_blocks = dm.B * nb
    ctx_per = dm.L // NAT_QBLOCK
    n_blocks = lat_blocks + dm.B * ctx_per
    w = dm.nat_w
    q_col, k_col, v_col = dm.off_nat_q // w, dm.off_nat_k // w, dm.off_nat_v // w
    ctx_row0 = dm.n_lat // dm.L

    def batch_of(g):
        return jnp.where(g < lat_blocks, g // nb, (g - lat_blocks) // ctx_per)

    def nbr(g, d):
        n = g % nb
        return jnp.where(g < lat_blocks, (g // nb) * nb + jnp.clip(n + d, 0, nb - 1), g)

    def kind(g):
        n = g % nb
        return jnp.where(g < lat_blocks, jnp.where(n == 0, 0, jnp.where(n == nb - 1, 2, 1)), 3)

    blk = lambda col, d: pl.BlockSpec((NAT_QBLOCK, w), lambda g: (nbr(g, d), col))
    ctx = lambda col: pl.BlockSpec((dm.L, w), lambda g: (ctx_row0 + batch_of(g), col))
    in_specs = [pl.BlockSpec((NAT_QBLOCK, w), lambda g: (g, q_col)),
                blk(k_col, -1), blk(k_col, 0), blk(k_col, 1),
                blk(v_col, -1), blk(v_col, 0), blk(v_col, 1),
                ctx(k_col), ctx(v_col),
                pl.BlockSpec((1, dm.nat_heads, NAT_QBLOCK, 3 * NAT_QBLOCK), lambda g: (kind(g), 0, 0, 0)),
                pl.BlockSpec((1, w), lambda g: (0, 0))]
    return pl.pallas_call(
        functools.partial(_nat_kernel, heads=dm.nat_heads),
        grid=(n_blocks,),
        in_specs=in_specs,
        out_specs=pl.BlockSpec((NAT_QBLOCK, w), lambda g: (g, 0)),
        out_shape=jax.ShapeDtypeStruct((dm.rows, w), BF16),
        scratch_shapes=[pltpu.VMEM((NAT_QBLOCK, w), F32)],
        compiler_params=_params(1),
        name="nbr_attn",
    )(u, u, u, u, u, u, u, u, u, bias_tab, gain)


CONV_TILE = 256
HALO = 16


def _conv_kernel(u_ref, prev_ref, next_ref, w_ref, b_ref, g_ref, o_ref, *, cw, tiles_per_seq, lat_tiles):
    i = pl.program_id(0)
    is_ctx = i >= lat_tiles
    n = i % tiles_per_seq
    has_prev = jnp.logical_and(jnp.logical_not(is_ctx), n != 0)
    has_next = jnp.logical_and(jnp.logical_not(is_ctx), n != tiles_per_seq - 1)
    x = u_ref[:, 0:cw].astype(F32)
    bg = u_ref[:, cw:2 * cw].astype(F32)
    cg = u_ref[:, 2 * cw:3 * cw].astype(F32)
    z = cg * x
    zp = (prev_ref[HALO - 1:HALO, 2 * cw:3 * cw].astype(F32) * prev_ref[HALO - 1:HALO, 0:cw].astype(F32))
    zn = (next_ref[0:1, 2 * cw:3 * cw].astype(F32) * next_ref[0:1, 0:cw].astype(F32))
    zp = jnp.where(has_prev, zp, 0.0)
    zn = jnp.where(has_next, zn, 0.0)
    row = lax.broadcasted_iota(jnp.int32, z.shape, 0)
    z_m1 = jnp.where(row == 0, zp, pltpu.roll(z, 1, 0))
    z_p1 = jnp.where(row == CONV_TILE - 1, zn, pltpu.roll(z, CONV_TILE - 1, 0))
    w = w_ref[...]
    conv = b_ref[...] + z_m1 * w[0:1] + z * w[1:2] + z_p1 * w[2:3]
    o_ref[...] = _rms_gain(bg * conv, g_ref[...])


def _conv(dm, u, w, b, gain):
    assert dm.L == CONV_TILE, "context sequences are one convolution tile"
    cw = dm.conv_w
    tiles = dm.rows // CONV_TILE
    lat_tiles = dm.n_lat // CONV_TILE
    col = dm.off_conv // (3 * cw)
    per_halo = CONV_TILE // HALO
    last_halo = dm.rows // HALO - 1
    kern = functools.partial(_conv_kernel, cw=cw, tiles_per_seq=dm.S // CONV_TILE, lat_tiles=lat_tiles)
    return pl.pallas_call(
        kern,
        grid=(tiles,),
        in_specs=[pl.BlockSpec((CONV_TILE, 3 * cw), lambda i: (i, col)),
                  pl.BlockSpec((HALO, 3 * cw), lambda i: (jnp.maximum(i * per_halo - 1, 0), col)),
                  pl.BlockSpec((HALO, 3 * cw), lambda i: (jnp.minimum((i + 1) * per_halo, last_halo), col)),
                  pl.BlockSpec((3, cw), lambda i: (0, 0)),
                  pl.BlockSpec((1, cw), lambda i: (0, 0)),
                  pl.BlockSpec((1, cw), lambda i: (0, 0))],
        out_specs=pl.BlockSpec((CONV_TILE, cw), lambda i: (i, 0)),
        out_shape=jax.ShapeDtypeStruct((dm.rows, cw), BF16),
        compiler_params=_params(1),
        name="gated_conv",
    )(u, u, u, w, b, gain)


def _layer_norm(y, g, b):
    mu = jnp.mean(y, axis=-1, keepdims=True)
    yc = y - mu
    var = jnp.mean(yc * yc, axis=-1, keepdims=True)
    return yc * lax.rsqrt(var + LN_EPS) * g + b


def _route(logits):
    lane = lax.broadcasted_iota(jnp.int32, logits.shape, 1).astype(F32)
    big = float(ROUTER_LANES)
    is_g = lane < N_GROUPS
    gl = jnp.where(is_g, logits, NEG_INF)
    gmax = jnp.max(gl, axis=-1, keepdims=True)
    ge = jnp.where(is_g, jnp.exp(gl - gmax), 0.0)
    gprob = ge / jnp.sum(ge, axis=-1, keepdims=True)
    g_p = jnp.max(gprob, axis=-1, keepdims=True)
    g_idx = jnp.min(jnp.where(jnp.logical_and(is_g, gprob == g_p), lane, big), axis=-1, keepdims=True)
    lo = N_GROUPS + g_idx * EXPERTS_PER_GROUP
    in_grp = jnp.logical_and(lane >= lo, lane < lo + EXPERTS_PER_GROUP)
    el = jnp.where(in_grp, logits, NEG_INF)
    v1 = jnp.max(el, axis=-1, keepdims=True)
    i1 = jnp.min(jnp.where(jnp.logical_and(in_grp, el == v1), lane, big), axis=-1, keepdims=True)
    el2 = jnp.where(lane == i1, NEG_INF, el)
    v2 = jnp.max(el2, axis=-1, keepdims=True)
    rest = jnp.logical_and(in_grp, lane != i1)
    i2 = jnp.min(jnp.where(jnp.logical_and(rest, el2 == v2), lane, big), axis=-1, keepdims=True)
    e2 = jnp.exp(v2 - v1)
    den = 1.0 + e2
    w1 = g_p / den
    w2 = g_p * (e2 / den)
    return jnp.where(lane == i1, w1, jnp.where(lane == i2, w2, 0.0))


def _ln1_kernel(h_ref, mix_ref, mod_ref, g_ref, b_ref, wr_ref, br_ref, h_out, t_out, gate_out, *, D, alpha):
    mod = mod_ref[0]
    y = alpha * h_ref[...] + mod[:, 2 * D:3 * D] * mix_ref[...]
    hn = _layer_norm(y, g_ref[...], b_ref[...])
    h_out[...] = hn
    t = hn * (1.0 + mod[:, 4 * D:5 * D]) + mod[:, 3 * D:4 * D]
    t_out[...] = t.astype(BF16)
    logits = jnp.dot(t, wr_ref[...], preferred_element_type=F32, precision=lax.Precision.HIGHEST) + br_ref[...]
    gate_out[...] = _route(logits)


def _ln1(dm, rows, h, mix, mod, g, b, wr, br):
    tr, D = dm.tr, dm.D
    row = lambda width: pl.BlockSpec((tr, width), lambda i: (i, 0))
    vec = lambda width: pl.BlockSpec((1, width), lambda i: (0, 0))
    return pl.pallas_call(
        functools.partial(_ln1_kernel, D=D, alpha=dm.alpha),
        grid=(rows // tr,),
        in_specs=[row(D), row(D), _mod_spec(dm, tr), vec(D), vec(D),
                  pl.BlockSpec((D, ROUTER_LANES), lambda i: (0, 0)), vec(ROUTER_LANES)],
        out_specs=[row(D), row(D), row(ROUTER_LANES)],
        out_shape=[jax.ShapeDtypeStruct((rows, D), F32), jax.ShapeDtypeStruct((rows, D), BF16),
                   jax.ShapeDtypeStruct((rows, ROUTER_LANES), F32)],
        compiler_params=_params(1),
        name="mix_residual_norm_route",
    )(h, mix, mod, g, b, wr, br)


def _ln2_kernel(h_ref, f_ref, mod_ref, modn_ref, g_ref, b_ref, h_out, a_out, *, D, alpha):
    mod = mod_ref[0]
    y = alpha * h_ref[...] + mod[:, 5 * D:6 * D] * f_ref[...]
    hn = _layer_norm(y, g_ref[...], b_ref[...])
    h_out[...] = hn
    if a_out is not None:
        modn = modn_ref[0]
        a_out[...] = (hn * (1.0 + modn[:, D:2 * D]) + modn[:, 0:D]).astype(BF16)


def _ln2_last_kernel(h_ref, f_ref, mod_ref, g_ref, b_ref, h_out, *, D, alpha):
    _ln2_kernel(h_ref, f_ref, mod_ref, None, g_ref, b_ref, h_out, None, D=D, alpha=alpha)


def _ln2(dm, rows, h, ffn, mod, mod_next, g, b):
    tr, D = dm.tr, dm.D
    row = pl.BlockSpec((tr, D), lambda i: (i, 0))
    vec = pl.BlockSpec((1, D), lambda i: (0, 0))
    if mod_next is None:
        return pl.pallas_call(
            functools.partial(_ln2_last_kernel, D=D, alpha=dm.alpha),
            grid=(rows // tr,),
            in_specs=[row, row, _mod_spec(dm, tr), vec, vec],
            out_specs=row,
            out_shape=jax.ShapeDtypeStruct((rows, D), F32),
            compiler_params=_params(1),
            name="ffn_residual_norm_last",
        )(h, ffn, mod, g, b), None
    return pl.pallas_call(
        functools.partial(_ln2_kernel, D=D, alpha=dm.alpha),
        grid=(rows // tr,),
        in_specs=[row, row, _mod_spec(dm, tr), _mod_spec(dm, tr), vec, vec],
        out_specs=[row, row],
        out_shape=[jax.ShapeDtypeStruct((rows, D), F32), jax.ShapeDtypeStruct((rows, D), BF16)],
        compiler_params=_params(1),
        name="ffn_residual_norm",
    )(h, ffn, mod, mod_next, g, b)


def _moe_up_kernel(t_ref, wg_ref, wu_ref, gate_ref, o_ref):
    e = pl.program_id(1)
    t = t_ref[...]
    a = jnp.dot(t, wg_ref[0], preferred_element_type=F32)
    b = jnp.dot(t, wu_ref[0], preferred_element_type=F32)
    gates = gate_ref[...]
    lane = lax.broadcasted_iota(jnp.int32, gates.shape, 1)
    gcol = jnp.sum(jnp.where(lane == e + N_GROUPS, gates, 0.0), axis=-1, keepdims=True)
    o_ref[...] = (a * jax.nn.sigmoid(a) * b * gcol).astype(BF16)


def _moe_up(dm, rows, t, wg, wu, gates):
    tm, D, F = dm.tm, dm.D, dm.d_expert
    return pl.pallas_call(
        _moe_up_kernel,
        grid=(rows // tm, N_EXPERTS),
        in_specs=[pl.BlockSpec((tm, D), lambda i, e: (i, 0)),
                  pl.BlockSpec((1, D, F), lambda i, e: (e, 0, 0)),
                  pl.BlockSpec((1, D, F), lambda i, e: (e, 0, 0)),
                  pl.BlockSpec((tm, ROUTER_LANES), lambda i, e: (i, 0))],
        out_specs=pl.BlockSpec((tm, F), lambda i, e: (i, e)),
        out_shape=jax.ShapeDtypeStruct((rows, N_EXPERTS * F), BF16),
        compiler_params=_params(2),
        name="experts_up",
    )(t, wg, wu, gates)


def kernel(x, c, ctx, c_ctx, w_ada, b_ada, w_in, conv_w, conv_b, attn_sink, nat_rpb, mix_norm_g, w_out,
           ln1_g, ln1_b, w_router_group, b_router_group, w_router_expert, b_router_expert,
           w_gate, w_up, w_down, ln2_g, ln2_b):
    B, S, D = x.shape
    L = ctx.shape[1]
    depth = w_in.shape[0]
    dm = Dims(B, S, L, D, depth, w_gate.shape[-1])

    xin = jnp.concatenate([c, c_ctx[None], jnp.zeros((MOD_ROWS - B - 1, D), F32)], axis=0)
    mods = _ada_all(xin, w_ada, b_ada).reshape(depth, MOD_ROWS, 1, 6 * D)

    cos_t, sin_t = _rope_tables(dm)
    swa_mask = _swa_mask_table()
    nat_bias = _nat_bias_tables(dm, nat_rpb)

    w_in_b = w_in.astype(BF16)
    w_out_b = w_out.astype(BF16)
    w_gate_b = w_gate.astype(BF16)
    w_up_b = w_up.astype(BF16)
    w_down_b = w_down.reshape(depth, N_EXPERTS * dm.d_expert, D).astype(BF16)
    pad = jnp.zeros((depth, D, ROUTER_LANES - N_GROUPS - N_EXPERTS), F32)
    w_route = jnp.concatenate([w_router_group, w_router_expert, pad], axis=-1)
    b_route = jnp.concatenate([b_router_group, b_router_expert, pad[:, 0]], axis=-1).reshape(depth, 1, ROUTER_LANES)

    h = jnp.concatenate([x.reshape(B * S, D), ctx.reshape(B * L, D)], axis=0)
    a = _modulate(dm, h, mods[0])
    for i in range(depth):
        last = i == depth - 1
        rows = dm.n_lat if last else dm.rows
        u = _inproj(dm, a, w_in_b[i], cos_t, sin_t)
        gain = mix_norm_g[i].reshape(1, D)
        y_swa = _swa(dm, u, attn_sink[i], swa_mask, gain[:, :dm.swa_q])
        y_conv = _conv(dm, u, conv_w[i], conv_b[i].reshape(1, -1), gain[:, dm.swa_q:dm.swa_q + dm.conv_w])
        y_nat = _nat(dm, u, nat_bias[i], gain[:, dm.swa_q + dm.conv_w:])
        mix = _matmul([y_swa, y_conv, y_nat], w_out_b[i], rows, dm.tm, 512, F32, "out_proj")
        h_mid, t, gates = _ln1(dm, rows, h, mix, mods[i], ln1_g[i].reshape(1, D), ln1_b[i].reshape(1, D),
                               w_route[i], b_route[i])
        hid = _moe_up(dm, rows, t, w_gate_b[i], w_up_b[i], gates)
        ffn = _matmul([hid], w_down_b[i], rows, dm.tm, 512, F32, "experts_down")
        h, a = _ln2(dm, rows, h_mid, ffn, mods[i], None if last else mods[i + 1],
                    ln2_g[i].reshape(1, D), ln2_b[i].reshape(1, D))
    return h.reshape(B, S, D)
```

```python
import functools

import numpy as np
import jax
import jax.numpy as jnp
from jax import lax
from jax.experimental import pallas as pl
from jax.experimental.pallas import tpu as pltpu

F32 = jnp.float32
BF16 = jnp.bfloat16

HEAD_DIM = 128
GRID_W = 64
ROPE_THETA = 10000.0
ROPE_FREQS = HEAD_DIM // 4
SWA_GROUP = 4
SWA_BLOCK = 128
NAT_ROWS = 8
NAT_COLS = 16
NAT_QROWS = 4
NAT_QBLOCK = NAT_QROWS * GRID_W
N_GROUPS = 4
EXPERTS_PER_GROUP = 4
N_EXPERTS = N_GROUPS * EXPERTS_PER_GROUP
ROUTER_LANES = 128
ROUTE_GROUP_LANE = 0
EXPERT_TILE = 512
LN_EPS = 1e-5
RMS_EPS = 1e-6
NEG_INF = -1e30
ATTN_SCALE = HEAD_DIM ** -0.5
MOD_ROWS = 8
VMEM_LIMIT = 56 * 1024 * 1024


def _params(n_axes):
    return pltpu.CompilerParams(dimension_semantics=("arbitrary",) * n_axes,
                                vmem_limit_bytes=VMEM_LIMIT)


class Dims:
    def __init__(self, B, S, L, D, depth, d_expert):
        self.B, self.S, self.L, self.D, self.depth, self.d_expert = B, S, L, D, depth, d_expert
        self.swa_q = D // 2
        self.swa_heads = self.swa_q // HEAD_DIM
        self.swa_kv_heads = self.swa_heads // SWA_GROUP
        self.swa_kv = self.swa_kv_heads * HEAD_DIM
        self.conv_w = D // 4
        self.nat_w = D // 4
        self.nat_heads = self.nat_w // HEAD_DIM
        self.off_swa_k = self.swa_q
        self.off_swa_v = self.off_swa_k + self.swa_kv
        self.off_conv = self.off_swa_v + self.swa_kv
        self.off_nat_q = self.off_conv + 3 * self.conv_w
        self.off_nat_k = self.off_nat_q + self.nat_w
        self.off_nat_v = self.off_nat_k + self.nat_w
        self.in_cols = self.off_nat_v + self.nat_w
        self.n_lat = B * S
        self.n_ctx = B * L
        self.rows = self.n_lat + self.n_ctx
        self.grid_rows = S // GRID_W
        self.packed_w = D // 2 + ROUTER_LANES
        assert self.rows % EXPERT_TILE == 0 and self.n_lat % EXPERT_TILE == 0
        self.alpha = (2.0 * depth) ** 0.25
        self.tm = 1024 if (S % 1024 == 0 and self.n_ctx % 1024 == 0) else 256
        self.tr = 256
        self.tn = self.swa_kv
        assert S % self.tm == 0 and self.n_ctx % self.tm == 0
        assert S % NAT_QBLOCK == 0 and L % NAT_QBLOCK == 0 and self.grid_rows >= 3 * NAT_QROWS
        assert S // SWA_BLOCK >= 3 and L % SWA_BLOCK == 0
        assert self.off_conv % (3 * self.conv_w) == 0
        assert B + 1 <= MOD_ROWS


def _ada_kernel(x_ref, w_ref, b_ref, o_ref):
    x = x_ref[...]
    act = x * jax.nn.sigmoid(x)
    a_hi = act.astype(BF16)
    a_lo = (act - a_hi.astype(F32)).astype(BF16)
    w = w_ref[0]
    w_hi = w.astype(BF16)
    w_lo = (w - w_hi.astype(F32)).astype(BF16)
    acc = jnp.dot(a_hi, w_hi, preferred_element_type=F32)
    acc += jnp.dot(a_lo, w_hi, preferred_element_type=F32)
    acc += jnp.dot(a_hi, w_lo, preferred_element_type=F32)
    o_ref[0] = acc + b_ref[0]


def _ada_all(xin, w_ada, b_ada):
    depth, D, n6 = w_ada.shape
    tn = 512
    return pl.pallas_call(
        _ada_kernel,
        grid=(depth, n6 // tn),
        in_specs=[pl.BlockSpec((MOD_ROWS, D), lambda l, j: (0, 0)),
                  pl.BlockSpec((1, D, tn), lambda l, j: (l, 0, j)),
                  pl.BlockSpec((1, 1, tn), lambda l, j: (l, 0, j))],
        out_specs=pl.BlockSpec((1, MOD_ROWS, tn), lambda l, j: (l, 0, j)),
        out_shape=jax.ShapeDtypeStruct((depth, MOD_ROWS, n6), F32),
        compiler_params=_params(2),
        name="ada_mod",
    )(xin, w_ada, b_ada.reshape(depth, 1, n6))


def _mod_spec(dm, tile):
    n6 = 6 * dm.D
    return pl.BlockSpec((1, 1, n6), lambda i, *_: (jnp.minimum(i * tile // dm.S, dm.B), 0, 0))


def _modulate_kernel(h_ref, mod_ref, o_ref, *, D):
    mod = mod_ref[0]
    o_ref[...] = (h_ref[...] * (1.0 + mod[:, D:2 * D]) + mod[:, 0:D]).astype(BF16)


def _modulate(dm, h, mod):
    tr = dm.tr
    return pl.pallas_call(
        functools.partial(_modulate_kernel, D=dm.D),
        grid=(dm.rows // tr,),
        in_specs=[pl.BlockSpec((tr, dm.D), lambda i: (i, 0)), _mod_spec(dm, tr)],
        out_specs=pl.BlockSpec((tr, dm.D), lambda i: (i, 0)),
        out_shape=jax.ShapeDtypeStruct((dm.rows, dm.D), BF16),
        compiler_params=_params(1),
        name="modulate_in",
    )(h, mod)


def _rope_tables(dm):
    t = np.arange(dm.S)
    pos = np.stack([t // GRID_W, t % GRID_W], axis=-1).astype(np.float32)
    inv_freq = jnp.asarray(ROPE_THETA, F32) ** (-jnp.arange(ROPE_FREQS, dtype=F32) / ROPE_FREQS)
    ang = jnp.asarray(pos)[:, :, None] * inv_freq
    cos, sin = jnp.cos(ang), jnp.sin(ang)
    cos_t = jnp.concatenate([cos, cos], axis=-1).reshape(dm.S, HEAD_DIM)
    sin_t = jnp.concatenate([-sin, sin], axis=-1).reshape(dm.S, HEAD_DIM)
    cos_t = jnp.concatenate([cos_t, jnp.ones((dm.tm, HEAD_DIM), F32)], axis=0)
    sin_t = jnp.concatenate([sin_t, jnp.zeros((dm.tm, HEAD_DIM), F32)], axis=0)
    return cos_t, sin_t


def _inproj_kernel(a_ref, w_ref, cos_ref, sin_ref, o_ref, *, tn, n_rope_q, n_rope, nat_q0, nat_q1):
    j = pl.program_id(1)
    acc = jnp.dot(a_ref[...], w_ref[...], preferred_element_type=F32)
    is_rope = j < n_rope
    is_natq = jnp.logical_and(j >= nat_q0, j < nat_q1)

    @pl.when(is_rope)
    def _():
        cos = cos_ref[...]
        sin = sin_ref[...]
        scale = jnp.where(j < n_rope_q, ATTN_SCALE, 1.0).astype(F32)
        lane = lax.broadcasted_iota(jnp.int32, cos.shape, 1)
        first_half = (lane % (2 * ROPE_FREQS)) < ROPE_FREQS
        for c in range(tn // HEAD_DIM):
            x = acc[:, c * HEAD_DIM:(c + 1) * HEAD_DIM]
            partner = jnp.where(first_half,
                                pltpu.roll(x, HEAD_DIM - ROPE_FREQS, 1),
                                pltpu.roll(x, ROPE_FREQS, 1))
            o_ref[:, c * HEAD_DIM:(c + 1) * HEAD_DIM] = ((x * cos + partner * sin) * scale).astype(BF16)

    @pl.when(is_natq)
    def _():
        o_ref[...] = (acc * ATTN_SCALE).astype(BF16)

    @pl.when(jnp.logical_not(jnp.logical_or(is_rope, is_natq)))
    def _():
        o_ref[...] = acc.astype(BF16)


def _inproj(dm, a, w_all, layer, cos_t, sin_t):
    tm, tn = dm.tm, dm.tn
    lat_tiles = dm.n_lat // tm
    per_seq = dm.S // tm

    def tab_map(i, j):
        return (jnp.where(i < lat_tiles, i % per_seq, per_seq), 0)

    kern = functools.partial(_inproj_kernel, tn=tn, n_rope_q=dm.swa_q // tn, n_rope=dm.off_swa_v // tn,
                             nat_q0=dm.off_nat_q // tn, nat_q1=dm.off_nat_k // tn)
    return pl.pallas_call(
        kern,
        grid=(dm.rows // tm, dm.in_cols // tn),
        in_specs=[pl.BlockSpec((tm, dm.D), lambda i, j: (i, 0)),
                  pl.BlockSpec((None, dm.D, tn), lambda i, j: (layer, 0, j)),
                  pl.BlockSpec((tm, HEAD_DIM), tab_map),
                  pl.BlockSpec((tm, HEAD_DIM), tab_map)],
        out_specs=pl.BlockSpec((tm, tn), lambda i, j: (i, j)),
        out_shape=jax.ShapeDtypeStruct((dm.rows, dm.in_cols), BF16),
        compiler_params=_params(2),
        name="in_proj",
    )(a, w_all, cos_t, sin_t)


def _mm_kernel(*refs, k_splits):
    a_refs, w_ref, o_ref = refs[:-2], refs[-2], refs[-1]
    acc = None
    k0 = 0
    for a_ref, kw in zip(a_refs, k_splits):
        part = jnp.dot(a_ref[...], w_ref[k0:k0 + kw, :], preferred_element_type=F32)
        acc = part if acc is None else acc + part
        k0 += kw
    o_ref[...] = acc.astype(o_ref.dtype)


def _matmul(a_list, w_all, layer, rows, tm, tn, out_dtype, name):
    _, K, N = w_all.shape
    k_splits = tuple(a.shape[1] for a in a_list)
    assert sum(k_splits) == K
    in_specs = [pl.BlockSpec((tm, kw), lambda i, j: (i, 0)) for kw in k_splits]
    in_specs.append(pl.BlockSpec((None, K, tn), lambda i, j: (layer, 0, j)))
    return pl.pallas_call(
        functools.partial(_mm_kernel, k_splits=k_splits),
        grid=(rows // tm, N // tn),
        in_specs=in_specs,
        out_specs=pl.BlockSpec((tm, tn), lambda i, j: (i, j)),
        out_shape=jax.ShapeDtypeStruct((rows, N), out_dtype),
        compiler_params=_params(2),
        name=name,
    )(*a_list, w_all)


def _dot_nt(a, b):
    return lax.dot_general(a, b, (((1,), (1,)), ((), ())), preferred_element_type=F32)


def _rms_gain(o, g):
    ms = jnp.mean(o * o, axis=-1, keepdims=True)
    return (o * lax.rsqrt(ms + RMS_EPS) * g).astype(BF16)


def _swa_mask_table():
    qi = np.arange(SWA_BLOCK)[:, None]
    kj = np.arange(SWA_BLOCK)[None, :]
    prev_ok = kj >= qi
    cur_ok = np.ones((SWA_BLOCK, SWA_BLOCK), bool)
    next_ok = kj <= qi
    none = np.zeros((SWA_BLOCK, SWA_BLOCK), bool)
    kinds = [np.concatenate([none, cur_ok, next_ok], 1),
             np.concatenate([prev_ok, cur_ok, next_ok], 1),
             np.concatenate([prev_ok, cur_ok, none], 1),
             np.concatenate([none, none, none], 1)]
    return jnp.asarray(np.where(np.stack(kinds), 0.0, NEG_INF).astype(np.float32))


def _swa_kernel(sink_ref, q_ref, kp_ref, kc_ref, kn_ref, vp_ref, vc_ref, vn_ref, kx_ref, vx_ref,
                mask_ref, g_ref, o_ref, acc_ref, *, kv_heads):
    mask = mask_ref[0]
    for hk in range(kv_heads):
        ks = slice(hk * HEAD_DIM, (hk + 1) * HEAD_DIM)
        q = jnp.concatenate([q_ref[:, (hk * SWA_GROUP + g) * HEAD_DIM:(hk * SWA_GROUP + g + 1) * HEAD_DIM]
                             for g in range(SWA_GROUP)], axis=0)
        kb = jnp.concatenate([kp_ref[:, ks], kc_ref[:, ks], kn_ref[:, ks]], axis=0)
        vb = jnp.concatenate([vp_ref[:, ks], vc_ref[:, ks], vn_ref[:, ks]], axis=0)
        s = _dot_nt(q, kb)
        s = (s.reshape(SWA_GROUP, SWA_BLOCK, 3 * SWA_BLOCK) + mask[None]).reshape(SWA_GROUP * SWA_BLOCK, 3 * SWA_BLOCK)
        sx = _dot_nt(q, kx_ref[:, ks])
        sink = jnp.concatenate([jnp.full((SWA_BLOCK, 1), sink_ref[hk * SWA_GROUP + g], F32)
                                for g in range(SWA_GROUP)], axis=0)
        m = jnp.maximum(jnp.maximum(jnp.max(s, axis=-1, keepdims=True), jnp.max(sx, axis=-1, keepdims=True)), sink)
        p = jnp.exp(s - m)
        px = jnp.exp(sx - m)
        denom = jnp.sum(p, axis=-1, keepdims=True) + jnp.sum(px, axis=-1, keepdims=True) + jnp.exp(sink - m)
        o = jnp.dot(p.astype(BF16), vb, preferred_element_type=F32)
        o += jnp.dot(px.astype(BF16), vx_ref[:, ks], preferred_element_type=F32)
        o = o / denom
        for g in range(SWA_GROUP):
            h = hk * SWA_GROUP + g
            acc_ref[:, h * HEAD_DIM:(h + 1) * HEAD_DIM] = o[g * SWA_BLOCK:(g + 1) * SWA_BLOCK]
    o_ref[...] = _rms_gain(acc_ref[...], g_ref[...])


def _swa(dm, u, sink, mask_tab, gain):
    nb = dm.S // SWA_BLOCK
    lat_blocks = dm.B * nb
    ctx_per = dm.L // SWA_BLOCK
    n_blocks = lat_blocks + dm.B * ctx_per
    kvw = dm.swa_kv
    k_col = dm.off_swa_k // kvw
    v_col = dm.off_swa_v // kvw
    ctx_row0 = dm.n_lat // dm.L

    def batch_of(g):
        return jnp.where(g < lat_blocks, g // nb, (g - lat_blocks) // ctx_per)

    def nbr(g, d):
        n = g % nb
        return jnp.where(g < lat_blocks, (g // nb) * nb + jnp.clip(n + d, 0, nb - 1), g)

    def kind(g):
        n = g % nb
        return jnp.where(g < lat_blocks, jnp.where(n == 0, 0, jnp.where(n == nb - 1, 2, 1)), 3)

    blk = lambda col, d: pl.BlockSpec((SWA_BLOCK, kvw), lambda g: (nbr(g, d), col))
    ctx = lambda col: pl.BlockSpec((dm.L, kvw), lambda g: (ctx_row0 + batch_of(g), col))
    in_specs = [pl.BlockSpec(memory_space=pltpu.SMEM),
                pl.BlockSpec((SWA_BLOCK, dm.swa_q), lambda g: (g, 0)),
                blk(k_col, -1), blk(k_col, 0), blk(k_col, 1),
                blk(v_col, -1), blk(v_col, 0), blk(v_col, 1),
                ctx(k_col), ctx(v_col),
                pl.BlockSpec((1, SWA_BLOCK, 3 * SWA_BLOCK), lambda g: (kind(g), 0, 0)),
                pl.BlockSpec((1, dm.swa_q), lambda g: (0, 0))]
    return pl.pallas_call(
        functools.partial(_swa_kernel, kv_heads=dm.swa_kv_heads),
        grid=(n_blocks,),
        in_specs=in_specs,
        out_specs=pl.BlockSpec((SWA_BLOCK, dm.swa_q), lambda g: (g, 0)),
        out_shape=jax.ShapeDtypeStruct((dm.rows, dm.swa_q), BF16),
        scratch_shapes=[pltpu.VMEM((SWA_BLOCK, dm.swa_q), F32)],
        compiler_params=_params(1),
        name="window_attn",
    )(sink, u, u, u, u, u, u, u, u, u, mask_tab, gain)


def _nat_bias_tables(dm, rpb_all):
    nq, nk = NAT_QROWS, 3 * NAT_QROWS
    i = np.arange(nq)[:, None]
    j = np.arange(nk)[None, :]
    row_ok = [(j >= nq) & (j < nq + NAT_ROWS) & (i >= 0),
              (j - i >= 0) & (j - i < NAT_ROWS),
              (j >= 0) & (j < NAT_ROWS) & (i >= 0)]
    dr = np.clip(j - i + (NAT_ROWS - 1) - nq, 0, 2 * NAT_ROWS - 2)
    t_row = np.zeros((3, 2 * NAT_ROWS - 1, nq, nk), np.float32)
    for k in range(3):
        for a in range(nq):
            for b in range(nk):
                if row_ok[k][a, b]:
                    t_row[k, dr[a, b], a, b] = 1.0
    cq = np.arange(GRID_W)[:, None]
    kc = np.arange(GRID_W)[None, :]
    cs = np.clip(cq - NAT_COLS // 2, 0, GRID_W - NAT_COLS)
    col_ok = (kc >= cs) & (kc < cs + NAT_COLS)
    dc = np.clip(kc - cq + NAT_COLS - 1, 0, 2 * NAT_COLS - 2)
    t_col = np.zeros((2 * NAT_COLS - 1, GRID_W, GRID_W), np.float32)
    for a in range(GRID_W):
        for b in range(GRID_W):
            if col_ok[a, b]:
                t_col[dc[a, b], a, b] = 1.0
    t_row, t_col = jnp.asarray(t_row), jnp.asarray(t_col)
    val = jnp.einsum('lhrc,trij,cqk->lthiqjk', rpb_all, t_row, t_col, precision=lax.Precision.HIGHEST)
    ok = jnp.einsum('trij,cqk->tiqjk', t_row, t_col) > 0.5
    bias = jnp.where(ok[None, :, None], val, NEG_INF)
    depth, _, H = bias.shape[:3]
    bias = bias.reshape(depth, 3, H, NAT_QBLOCK, 3 * NAT_QBLOCK)
    none = jnp.full((depth, 1, H, NAT_QBLOCK, 3 * NAT_QBLOCK), NEG_INF, F32)
    return jnp.concatenate([bias, none], axis=1)


def _nat_kernel(q_ref, kp_ref, kc_ref, kn_ref, vp_ref, vc_ref, vn_ref, kx_ref, vx_ref,
                bias_ref, g_ref, o_ref, acc_ref, *, heads):
    for h in range(heads):
        hs = slice(h * HEAD_DIM, (h + 1) * HEAD_DIM)
        q = q_ref[:, hs]
        kb = jnp.concatenate([kp_ref[:, hs], kc_ref[:, hs], kn_ref[:, hs]], axis=0)
        vb = jnp.concatenate([vp_ref[:, hs], vc_ref[:, hs], vn_ref[:, hs]], axis=0)
        s = _dot_nt(q, kb) + bias_ref[0, h]
        sx = _dot_nt(q, kx_ref[:, hs])
        m = jnp.maximum(jnp.max(s, axis=-1, keepdims=True), jnp.max(sx, axis=-1, keepdims=True))
        p = jnp.exp(s - m)
        px = jnp.exp(sx - m)
        denom = jnp.sum(p, axis=-1, keepdims=True) + jnp.sum(px, axis=-1, keepdims=True)
        o = jnp.dot(p.astype(BF16), vb, preferred_element_type=F32)
        o += jnp.dot(px.astype(BF16), vx_ref[:, hs], preferred_element_type=F32)
        acc_ref[:, hs] = o / denom
    o_ref[...] = _rms_gain(acc_ref[...], g_ref[...])


def _nat(dm, u, bias_tab, gain):
    nb = dm.S // NAT_QBLOCK
    lat_blocks = dm.B * nb
    ctx_per = dm.L // NAT_QBLOCK
    n_blocks = lat_blocks + dm.B * ctx_per
    w = dm.nat_w
    q_col, k_col, v_col = dm.off_nat_q // w, dm.off_nat_k // w, dm.off_nat_v // w
    ctx_row0 = dm.n_lat // dm.L

    def batch_of(g):
        return jnp.where(g < lat_blocks, g // nb, (g - lat_blocks) // ctx_per)

    def nbr(g, d):
        n = g % nb
        return jnp.where(g < lat_blocks, (g // nb) * nb + jnp.clip(n + d, 0, nb - 1), g)

    def kind(g):
        n = g % nb
        return jnp.where(g < lat_blocks, jnp.where(n == 0, 0, jnp.where(n == nb - 1, 2, 1)), 3)

    blk = lambda col, d: pl.BlockSpec((NAT_QBLOCK, w), lambda g: (nbr(g, d), col))
    ctx = lambda col: pl.BlockSpec((dm.L, w), lambda g: (ctx_row0 + batch_of(g), col))
    in_specs = [pl.BlockSpec((NAT_QBLOCK, w), lambda g: (g, q_col)),
                blk(k_col, -1), blk(k_col, 0), blk(k_col, 1),
                blk(v_col, -1), blk(v_col, 0), blk(v_col, 1),
                ctx(k_col), ctx(v_col),
                pl.BlockSpec((1, dm.nat_heads, NAT_QBLOCK, 3 * NAT_QBLOCK), lambda g: (kind(g), 0, 0, 0)),
                pl.BlockSpec((1, w), lambda g: (0, 0))]
    return pl.pallas_call(
        functools.partial(_nat_kernel, heads=dm.nat_heads),
        grid=(n_blocks,),
        in_specs=in_specs,
        out_specs=pl.BlockSpec((NAT_QBLOCK, w), lambda g: (g, 0)),
        out_shape=jax.ShapeDtypeStruct((dm.rows, w), BF16),
        scratch_shapes=[pltpu.VMEM((NAT_QBLOCK, w), F32)],
        compiler_params=_params(1),
        name="nbr_attn",
    )(u, u, u, u, u, u, u, u, u, bias_tab, gain)


CONV_TILE = 256
HALO = 16


def _conv_kernel(u_ref, prev_ref, next_ref, w_ref, b_ref, g_ref, o_ref, *, cw, tiles_per_seq, lat_tiles):
    i = pl.program_id(0)
    is_ctx = i >= lat_tiles
    n = i % tiles_per_seq
    has_prev = jnp.logical_and(jnp.logical_not(is_ctx), n != 0)
    has_next = jnp.logical_and(jnp.logical_not(is_ctx), n != tiles_per_seq - 1)
    x = u_ref[:, 0:cw].astype(F32)
    bg = u_ref[:, cw:2 * cw].astype(F32)
    cg = u_ref[:, 2 * cw:3 * cw].astype(F32)
    z = cg * x
    zp = (prev_ref[HALO - 1:HALO, 2 * cw:3 * cw].astype(F32) * prev_ref[HALO - 1:HALO, 0:cw].astype(F32))
    zn = (next_ref[0:1, 2 * cw:3 * cw].astype(F32) * next_ref[0:1, 0:cw].astype(F32))
    zp = jnp.where(has_prev, zp, 0.0)
    zn = jnp.where(has_next, zn, 0.0)
    row = lax.broadcasted_iota(jnp.int32, z.shape, 0)
    z_m1 = jnp.where(row == 0, zp, pltpu.roll(z, 1, 0))
    z_p1 = jnp.where(row == CONV_TILE - 1, zn, pltpu.roll(z, CONV_TILE - 1, 0))
    w = w_ref[...]
    conv = b_ref[...] + z_m1 * w[0:1] + z * w[1:2] + z_p1 * w[2:3]
    o_ref[...] = _rms_gain(bg * conv, g_ref[...])


def _conv(dm, u, w, b, gain):
    assert dm.L == CONV_TILE, "context sequences are one convolution tile"
    cw = dm.conv_w
    tiles = dm.rows // CONV_TILE
    lat_tiles = dm.n_lat // CONV_TILE
    col = dm.off_conv // (3 * cw)
    per_halo = CONV_TILE // HALO
    last_halo = dm.rows // HALO - 1
    kern = functools.partial(_conv_kernel, cw=cw, tiles_per_seq=dm.S // CONV_TILE, lat_tiles=lat_tiles)
    return pl.pallas_call(
        kern,
        grid=(tiles,),
        in_specs=[pl.BlockSpec((CONV_TILE, 3 * cw), lambda i: (i, col)),
                  pl.BlockSpec((HALO, 3 * cw), lambda i: (jnp.maximum(i * per_halo - 1, 0), col)),
                  pl.BlockSpec((HALO, 3 * cw), lambda i: (jnp.minimum((i + 1) * per_halo, last_halo), col)),
                  pl.BlockSpec((3, cw), lambda i: (0, 0)),
                  pl.BlockSpec((1, cw), lambda i: (0, 0)),
                  pl.BlockSpec((1, cw), lambda i: (0, 0))],
        out_specs=pl.BlockSpec((CONV_TILE, cw), lambda i: (i, 0)),
        out_shape=jax.ShapeDtypeStruct((dm.rows, cw), BF16),
        compiler_params=_params(1),
        name="gated_conv",
    )(u, u, u, w, b, gain)


def _layer_norm(y, g, b):
    mu = jnp.mean(y, axis=-1, keepdims=True)
    yc = y - mu
    var = jnp.mean(yc * yc, axis=-1, keepdims=True)
    return yc * lax.rsqrt(var + LN_EPS) * g + b


def _route(logits):
    lane = lax.broadcasted_iota(jnp.int32, logits.shape, 1).astype(F32)
    big = float(ROUTER_LANES)
    is_g = lane < N_GROUPS
    gl = jnp.where(is_g, logits, NEG_INF)
    gmax = jnp.max(gl, axis=-1, keepdims=True)
    ge = jnp.where(is_g, jnp.exp(gl - gmax), 0.0)
    gprob = ge / jnp.sum(ge, axis=-1, keepdims=True)
    g_p = jnp.max(gprob, axis=-1, keepdims=True)
    g_idx = jnp.min(jnp.where(jnp.logical_and(is_g, gprob == g_p), lane, big), axis=-1, keepdims=True)
    lo = N_GROUPS + g_idx * EXPERTS_PER_GROUP
    in_grp = jnp.logical_and(lane >= lo, lane < lo + EXPERTS_PER_GROUP)
    el = jnp.where(in_grp, logits, NEG_INF)
    v1 = jnp.max(el, axis=-1, keepdims=True)
    i1 = jnp.min(jnp.where(jnp.logical_and(in_grp, el == v1), lane, big), axis=-1, keepdims=True)
    el2 = jnp.where(lane == i1, NEG_INF, el)
    v2 = jnp.max(el2, axis=-1, keepdims=True)
    rest = jnp.logical_and(in_grp, lane != i1)
    i2 = jnp.min(jnp.where(jnp.logical_and(rest, el2 == v2), lane, big), axis=-1, keepdims=True)
    e2 = jnp.exp(v2 - v1)
    den = 1.0 + e2
    w1 = g_p / den
    w2 = g_p * (e2 / den)
    gates = jnp.where(lane == i1, w1, jnp.where(lane == i2, w2, 0.0))
    return jnp.where(lane == ROUTE_GROUP_LANE, g_idx, gates)


HIGH_HALF = 0xFFFF0000


def _pack_bf16_pairs(x):
    w = x.shape[1] // 2
    hi = lax.bitcast_convert_type(x[:, :w].astype(BF16).astype(F32), jnp.uint32)
    lo = lax.bitcast_convert_type(x[:, w:].astype(BF16).astype(F32), jnp.uint32)
    return hi | (lo >> 16)


def _unpack_bf16_pairs(p):
    hi = lax.bitcast_convert_type(p & jnp.uint32(HIGH_HALF), F32)
    lo = lax.bitcast_convert_type(p << 16, F32)
    return hi, lo


def _ln1_kernel(h_ref, mix_ref, mod_ref, g_ref, b_ref, wr_ref, br_ref, h_out, t_out, route_out, *, D, alpha):
    mod = mod_ref[0]
    y = alpha * h_ref[...] + mod[:, 2 * D:3 * D] * mix_ref[...]
    hn = _layer_norm(y, g_ref[...], b_ref[...])
    h_out[...] = hn
    t = hn * (1.0 + mod[:, 4 * D:5 * D]) + mod[:, 3 * D:4 * D]
    logits = jnp.dot(t, wr_ref[...], preferred_element_type=F32, precision=lax.Precision.HIGHEST) + br_ref[...]
    route = _route(logits)
    route_out[...] = route
    t_out[:, 0:D // 2] = _pack_bf16_pairs(t)
    t_out[:, D // 2:] = lax.bitcast_convert_type(route, jnp.uint32)


def _ln1(dm, rows, h, mix, mod, g, b, wr, br):
    tr, D = dm.tr, dm.D
    row = lambda width: pl.BlockSpec((tr, width), lambda i: (i, 0))
    vec = lambda width: pl.BlockSpec((1, width), lambda i: (0, 0))
    return pl.pallas_call(
        functools.partial(_ln1_kernel, D=D, alpha=dm.alpha),
        grid=(rows // tr,),
        in_specs=[row(D), row(D), _mod_spec(dm, tr), vec(D), vec(D),
                  pl.BlockSpec((D, ROUTER_LANES), lambda i: (0, 0)), vec(ROUTER_LANES)],
        out_specs=[row(D), row(dm.packed_w), row(ROUTER_LANES)],
        out_shape=[jax.ShapeDtypeStruct((rows, D), F32), jax.ShapeDtypeStruct((rows, dm.packed_w), jnp.uint32),
                   jax.ShapeDtypeStruct((rows, ROUTER_LANES), F32)],
        compiler_params=_params(1),
        name="mix_residual_norm_route",
    )(h, mix, mod, g, b, wr, br)


def _ln2_kernel(h_ref, f_ref, mod_ref, modn_ref, g_ref, b_ref, h_out, a_out, *, D, alpha):
    mod = mod_ref[0]
    f_hi, f_lo = _unpack_bf16_pairs(f_ref[...])
    ffn = jnp.concatenate([f_hi, f_lo], axis=1)
    y = alpha * h_ref[...] + mod[:, 5 * D:6 * D] * ffn
    hn = _layer_norm(y, g_ref[...], b_ref[...])
    h_out[...] = hn
    if a_out is not None:
        modn = modn_ref[0]
        a_out[...] = (hn * (1.0 + modn[:, D:2 * D]) + modn[:, 0:D]).astype(BF16)


def _ln2_last_kernel(h_ref, f_ref, mod_ref, g_ref, b_ref, h_out, *, D, alpha):
    _ln2_kernel(h_ref, f_ref, mod_ref, None, g_ref, b_ref, h_out, None, D=D, alpha=alpha)


def _ln2(dm, rows, h, ffn, mod, mod_next, g, b):
    tr, D = dm.tr, dm.D
    row = pl.BlockSpec((tr, D), lambda i: (i, 0))
    packed = pl.BlockSpec((tr, D // 2), lambda i: (i, 0))
    vec = pl.BlockSpec((1, D), lambda i: (0, 0))
    if mod_next is None:
        return pl.pallas_call(
            functools.partial(_ln2_last_kernel, D=D, alpha=dm.alpha),
            grid=(rows // tr,),
            in_specs=[row, packed, _mod_spec(dm, tr), vec, vec],
            out_specs=row,
            out_shape=jax.ShapeDtypeStruct((rows, D), F32),
            compiler_params=_params(1),
            name="ffn_residual_norm_last",
        )(h, ffn, mod, g, b), None
    return pl.pallas_call(
        functools.partial(_ln2_kernel, D=D, alpha=dm.alpha),
        grid=(rows // tr,),
        in_specs=[row, packed, _mod_spec(dm, tr), _mod_spec(dm, tr), vec, vec],
        out_specs=[row, row],
        out_shape=[jax.ShapeDtypeStruct((rows, D), F32), jax.ShapeDtypeStruct((rows, D), BF16)],
        compiler_params=_params(1),
        name="ffn_residual_norm",
    )(h, ffn, mod, mod_next, g, b)


GATHER_CHUNK = 512


def _gather_kernel(idx_ref, src_hbm, out_hbm, sem, *, n_chunks):
    c = pl.program_id(0)
    base = c * GATHER_CHUNK

    def issue(j, carry):
        row = idx_ref[base + j]
        pltpu.make_async_copy(src_hbm.at[pl.ds(row, 1)], out_hbm.at[pl.ds(base + j, 1)], sem.at[c % 2]).start()
        return carry

    lax.fori_loop(0, GATHER_CHUNK, issue, 0, unroll=8)

    def wait_chunk(slot):
        pltpu.make_async_copy(src_hbm.at[pl.ds(0, GATHER_CHUNK)], out_hbm.at[pl.ds(0, GATHER_CHUNK)],
                              sem.at[slot]).wait()

    @pl.when(c > 0)
    def _():
        wait_chunk((c - 1) % 2)

    @pl.when(c == n_chunks - 1)
    def _():
        wait_chunk(c % 2)


def _row_gather(idx, table, name):
    n_out = idx.shape[0]
    n_chunks = n_out // GATHER_CHUNK
    assert n_chunks * GATHER_CHUNK == n_out
    return pl.pallas_call(
        functools.partial(_gather_kernel, n_chunks=n_chunks),
        grid_spec=pltpu.PrefetchScalarGridSpec(
            num_scalar_prefetch=1,
            grid=(n_chunks,),
            in_specs=[pl.BlockSpec(memory_space=pl.ANY)],
            out_specs=pl.BlockSpec(memory_space=pl.ANY),
            scratch_shapes=[pltpu.SemaphoreType.DMA((2,))]),
        out_shape=jax.ShapeDtypeStruct((n_out, table.shape[1]), table.dtype),
        compiler_params=_params(1),
        name=name,
    )(idx, table)


def _dispatch_plan(rows, route):
    tm = EXPERT_TILE
    n_tiles = rows // tm + N_GROUPS
    i32 = jnp.int32
    gidx = route[:, ROUTE_GROUP_LANE].astype(i32)
    onehot = (gidx[:, None] == jnp.arange(N_GROUPS, dtype=i32)[None]).astype(i32)
    csum = jnp.cumsum(onehot, axis=0)
    counts = csum[-1]
    rank = jnp.sum(onehot * csum, axis=1) - 1
    padded = (counts + tm - 1) // tm * tm
    pstart = jnp.cumsum(padded) - padded
    ustart = jnp.cumsum(counts) - counts
    total = jnp.sum(padded)
    pos = jnp.sum(onehot * pstart[None], axis=1) + rank
    order = jnp.argsort(gidx, stable=True).astype(i32)
    slot = jnp.arange(n_tiles * tm, dtype=i32)
    sgrp = jnp.sum((slot[:, None] >= pstart[None, 1:]).astype(i32), axis=1)
    shot = (sgrp[:, None] == jnp.arange(N_GROUPS, dtype=i32)[None]).astype(i32)
    k = slot - jnp.sum(shot * pstart[None], axis=1)
    valid = jnp.logical_and(k < jnp.sum(shot * counts[None], axis=1), slot < total)
    sorted_at = jnp.clip(jnp.sum(shot * ustart[None], axis=1) + k, 0, rows - 1)
    src = jnp.where(valid, order[sorted_at], slot % rows)
    tile_slot = jnp.arange(n_tiles, dtype=i32) * tm
    tgrp = jnp.where(tile_slot < total, sgrp[::tm], N_GROUPS)
    return src, pos.astype(i32), tgrp.astype(i32)


def _experts_kernel(tgrp_ref, ts_ref, wg_ref, wu_ref, wd_ref, o_ref, tb_ref, gate_ref, acc_ref, *, D):
    i = pl.program_id(0)
    e = pl.program_id(1)
    grp = tgrp_ref[i]
    active = grp < N_GROUPS
    half = D // 2

    @pl.when(e == 0)
    def _():
        hi, lo = _unpack_bf16_pairs(ts_ref[:, 0:half])
        tb_ref[:, 0:half] = hi.astype(BF16)
        tb_ref[:, half:] = lo.astype(BF16)
        gate_ref[...] = lax.bitcast_convert_type(ts_ref[:, half:], F32)

    @pl.when(active)
    def _():
        t = tb_ref[...]
        a = jnp.dot(t, wg_ref[...], preferred_element_type=F32)
        b = jnp.dot(t, wu_ref[...], preferred_element_type=F32)
        gates = gate_ref[...]
        lane = lax.broadcasted_iota(jnp.int32, gates.shape, 1)
        gcol = jnp.sum(jnp.where(lane == N_GROUPS + grp * EXPERTS_PER_GROUP + e, gates, 0.0),
                       axis=-1, keepdims=True)
        hid = (a * jax.nn.sigmoid(a) * b * gcol).astype(BF16)
        part = jnp.dot(hid, wd_ref[...], preferred_element_type=F32)

        @pl.when(e == 0)
        def _():
            acc_ref[...] = part

        @pl.when(e > 0)
        def _():
            acc_ref[...] += part

    last = e == EXPERTS_PER_GROUP - 1

    @pl.when(jnp.logical_and(last, active))
    def _():
        o_ref[...] = _pack_bf16_pairs(acc_ref[...])

    @pl.when(jnp.logical_and(last, jnp.logical_not(active)))
    def _():
        o_ref[...] = jnp.zeros(o_ref.shape, o_ref.dtype)


def _experts(dm, layer, tgrp, t_sorted, wg_all, wu_all, wd_all):
    tm, D, F = EXPERT_TILE, dm.D, dm.d_expert
    n_tiles = t_sorted.shape[0] // tm

    def w_map(i, e, tg):
        return (layer, jnp.minimum(tg[i], N_GROUPS - 1) * EXPERTS_PER_GROUP + e, 0, 0)

    return pl.pallas_call(
        functools.partial(_experts_kernel, D=D),
        grid_spec=pltpu.PrefetchScalarGridSpec(
            num_scalar_prefetch=1,
            grid=(n_tiles, EXPERTS_PER_GROUP),
            in_specs=[pl.BlockSpec((tm, dm.packed_w), lambda i, e, tg: (i, 0)),
                      pl.BlockSpec((None, None, D, F), w_map),
                      pl.BlockSpec((None, None, D, F), w_map),
                      pl.BlockSpec((None, None, F, D), w_map)],
            out_specs=pl.BlockSpec((tm, D // 2), lambda i, e, tg: (i, 0)),
            scratch_shapes=[pltpu.VMEM((tm, D), BF16), pltpu.VMEM((tm, ROUTER_LANES), F32),
                            pltpu.VMEM((tm, D), F32)]),
        out_shape=jax.ShapeDtypeStruct((n_tiles * tm, D // 2), jnp.uint32),
        compiler_params=_params(2),
        name="experts",
    )(tgrp, t_sorted, wg_all, wu_all, wd_all)


def kernel(x, c, ctx, c_ctx, w_ada, b_ada, w_in, conv_w, conv_b, attn_sink, nat_rpb, mix_norm_g, w_out,
           ln1_g, ln1_b, w_router_group, b_router_group, w_router_expert, b_router_expert,
           w_gate, w_up, w_down, ln2_g, ln2_b):
    B, S, D = x.shape
    L = ctx.shape[1]
    depth = w_in.shape[0]
    dm = Dims(B, S, L, D, depth, w_gate.shape[-1])

    xin = jnp.concatenate([c, c_ctx[None], jnp.zeros((MOD_ROWS - B - 1, D), F32)], axis=0)
    mods = _ada_all(xin, w_ada, b_ada).reshape(depth, MOD_ROWS, 1, 6 * D)

    cos_t, sin_t = _rope_tables(dm)
    swa_mask = _swa_mask_table()
    nat_bias = _nat_bias_tables(dm, nat_rpb)

    w_in_b = w_in.astype(BF16)
    w_out_b = w_out.astype(BF16)
    w_gate_b = w_gate.astype(BF16)
    w_up_b = w_up.astype(BF16)
    w_down_b = w_down.astype(BF16)
    pad = jnp.zeros((depth, D, ROUTER_LANES - N_GROUPS - N_EXPERTS), F32)
    w_route = jnp.concatenate([w_router_group, w_router_expert, pad], axis=-1)
    b_route = jnp.concatenate([b_router_group, b_router_expert, pad[:, 0]], axis=-1).reshape(depth, 1, ROUTER_LANES)

    h = jnp.concatenate([x.reshape(B * S, D), ctx.reshape(B * L, D)], axis=0)
    a = _modulate(dm, h, mods[0])
    for i in range(depth):
        last = i == depth - 1
        rows = dm.n_lat if last else dm.rows
        u = _inproj(dm, a, w_in_b, i, cos_t, sin_t)
        gain = mix_norm_g[i].reshape(1, D)
        y_swa = _swa(dm, u, attn_sink[i], swa_mask, gain[:, :dm.swa_q])
        y_conv = _conv(dm, u, conv_w[i], conv_b[i].reshape(1, -1), gain[:, dm.swa_q:dm.swa_q + dm.conv_w])
        y_nat = _nat(dm, u, nat_bias[i], gain[:, dm.swa_q + dm.conv_w:])
        mix = _matmul([y_swa, y_conv, y_nat], w_out_b, i, rows, dm.tm, 512, F32, "out_proj")
        h_mid, t_packed, route = _ln1(dm, rows, h, mix, mods[i], ln1_g[i].reshape(1, D), ln1_b[i].reshape(1, D),
                                      w_route[i], b_route[i])
        src, pos, tgrp = _dispatch_plan(rows, route)
        t_sorted = _row_gather(src, t_packed, "dispatch_rows")
        ffn_sorted = _experts(dm, i, tgrp, t_sorted, w_gate_b, w_up_b, w_down_b)
        ffn = _row_gather(pos, ffn_sorted, "combine_rows")
        h, a = _ln2(dm, rows, h_mid, ffn, mods[i], None if last else mods[i + 1],
                    ln2_g[i].reshape(1, D), ln2_b[i].reshape(1, D))
    return h.reshape(B, S, D)
```

```python
import functools

import numpy as np
import jax
import jax.numpy as jnp
from jax import lax
from jax.experimental import pallas as pl
from jax.experimental.pallas import tpu as pltpu

F32 = jnp.float32
BF16 = jnp.bfloat16

HEAD_DIM = 128
LANES = 128
GRID_W = 64
ROPE_THETA = 10000.0
ROPE_FREQS = HEAD_DIM // 4
SWA_GROUP = 4
SWA_BLOCK = 128
NAT_ROWS = 8
NAT_COLS = 16
NAT_QROWS = 4
NAT_QBLOCK = NAT_QROWS * GRID_W
N_GROUPS = 4
EXPERTS_PER_GROUP = 4
N_EXPERTS = N_GROUPS * EXPERTS_PER_GROUP
ROUTER_LANES = 128
ROUTE_GROUP_LANE = 0
EXPERT_TILE = 512
LN_EPS = 1e-5
RMS_EPS = 1e-6
NEG_INF = -1e30
ATTN_SCALE = HEAD_DIM ** -0.5
MOD_ROWS = 8
VMEM_LIMIT = 56 * 1024 * 1024


def _params(n_axes):
    return pltpu.CompilerParams(dimension_semantics=("arbitrary",) * n_axes,
                                vmem_limit_bytes=VMEM_LIMIT)


class Dims:
    def __init__(self, B, S, L, D, depth, d_expert):
        self.B, self.S, self.L, self.D, self.depth, self.d_expert = B, S, L, D, depth, d_expert
        self.swa_q = D // 2
        self.swa_heads = self.swa_q // HEAD_DIM
        self.swa_kv_heads = self.swa_heads // SWA_GROUP
        self.swa_kv = self.swa_kv_heads * HEAD_DIM
        self.conv_w = D // 4
        self.nat_w = D // 4
        self.nat_heads = self.nat_w // HEAD_DIM
        self.off_swa_k = self.swa_q
        self.off_swa_v = self.off_swa_k + self.swa_kv
        self.off_conv = self.off_swa_v + self.swa_kv
        self.off_nat_q = self.off_conv + 3 * self.conv_w
        self.off_nat_k = self.off_nat_q + self.nat_w
        self.off_nat_v = self.off_nat_k + self.nat_w
        self.in_cols = self.off_nat_v + self.nat_w
        self.n_lat = B * S
        self.n_ctx = B * L
        self.rows = self.n_lat + self.n_ctx
        self.grid_rows = S // GRID_W
        self.packed_w = D // 2 + ROUTER_LANES
        assert self.rows % EXPERT_TILE == 0 and self.n_lat % EXPERT_TILE == 0
        self.alpha = (2.0 * depth) ** 0.25
        self.tm = 1024 if (S % 1024 == 0 and self.n_ctx % 1024 == 0) else 256
        self.tr = 256
        self.tn = self.swa_kv
        assert S % self.tm == 0 and self.n_ctx % self.tm == 0
        assert S % NAT_QBLOCK == 0 and L % NAT_QBLOCK == 0 and self.grid_rows >= 3 * NAT_QROWS
        assert S // SWA_BLOCK >= 3 and L % SWA_BLOCK == 0
        assert self.off_conv % (3 * self.conv_w) == 0
        assert B + 1 <= MOD_ROWS


def _ada_kernel(x_ref, w_ref, b_ref, o_ref):
    x = x_ref[...]
    act = x * jax.nn.sigmoid(x)
    a_hi = act.astype(BF16)
    a_lo = (act - a_hi.astype(F32)).astype(BF16)
    w = w_ref[0]
    w_hi = w.astype(BF16)
    w_lo = (w - w_hi.astype(F32)).astype(BF16)
    acc = jnp.dot(a_hi, w_hi, preferred_element_type=F32)
    acc += jnp.dot(a_lo, w_hi, preferred_element_type=F32)
    acc += jnp.dot(a_hi, w_lo, preferred_element_type=F32)
    o_ref[0] = acc + b_ref[0]


def _ada_all(xin, w_ada, b_ada):
    depth, D, n6 = w_ada.shape
    tn = 512
    return pl.pallas_call(
        _ada_kernel,
        grid=(depth, n6 // tn),
        in_specs=[pl.BlockSpec((MOD_ROWS, D), lambda l, j: (0, 0)),
                  pl.BlockSpec((1, D, tn), lambda l, j: (l, 0, j)),
                  pl.BlockSpec((1, 1, tn), lambda l, j: (l, 0, j))],
        out_specs=pl.BlockSpec((1, MOD_ROWS, tn), lambda l, j: (l, 0, j)),
        out_shape=jax.ShapeDtypeStruct((depth, MOD_ROWS, n6), F32),
        compiler_params=_params(2),
        name="ada_mod",
    )(xin, w_ada, b_ada.reshape(depth, 1, n6))


def _mod_spec(dm, tile):
    n6 = 6 * dm.D
    return pl.BlockSpec((1, 1, n6), lambda i, *_: (jnp.minimum(i * tile // dm.S, dm.B), 0, 0))


def _modulate_kernel(h_ref, mod_ref, o_ref, *, D):
    mod = mod_ref[0]
    o_ref[...] = (h_ref[...] * (1.0 + mod[:, D:2 * D]) + mod[:, 0:D]).astype(BF16)


def _modulate(dm, h, mod):
    tr = dm.tr
    return pl.pallas_call(
        functools.partial(_modulate_kernel, D=dm.D),
        grid=(dm.rows // tr,),
        in_specs=[pl.BlockSpec((tr, dm.D), lambda i: (i, 0)), _mod_spec(dm, tr)],
        out_specs=pl.BlockSpec((tr, dm.D), lambda i: (i, 0)),
        out_shape=jax.ShapeDtypeStruct((dm.rows, dm.D), BF16),
        compiler_params=_params(1),
        name="modulate_in",
    )(h, mod)


def _rope_tables(dm):
    t = np.arange(dm.S)
    pos = np.stack([t // GRID_W, t % GRID_W], axis=-1).astype(np.float32)
    inv_freq = jnp.asarray(ROPE_THETA, F32) ** (-jnp.arange(ROPE_FREQS, dtype=F32) / ROPE_FREQS)
    ang = jnp.asarray(pos)[:, :, None] * inv_freq
    cos, sin = jnp.cos(ang), jnp.sin(ang)
    cos_t = jnp.concatenate([cos, cos], axis=-1).reshape(dm.S, HEAD_DIM)
    sin_t = jnp.concatenate([-sin, sin], axis=-1).reshape(dm.S, HEAD_DIM)
    cos_t = jnp.concatenate([cos_t, jnp.ones((dm.tm, HEAD_DIM), F32)], axis=0)
    sin_t = jnp.concatenate([sin_t, jnp.zeros((dm.tm, HEAD_DIM), F32)], axis=0)
    return cos_t, sin_t


def _inproj_kernel(a_ref, w_ref, cos_ref, sin_ref, o_ref, *, tn, n_rope_q, n_rope, nat_q0, nat_q1):
    j = pl.program_id(1)
    acc = jnp.dot(a_ref[...], w_ref[...], preferred_element_type=F32)
    is_rope = j < n_rope
    is_natq = jnp.logical_and(j >= nat_q0, j < nat_q1)

    @pl.when(is_rope)
    def _():
        cos = cos_ref[...]
        sin = sin_ref[...]
        scale = jnp.where(j < n_rope_q, ATTN_SCALE, 1.0).astype(F32)
        lane = lax.broadcasted_iota(jnp.int32, cos.shape, 1)
        first_half = (lane % (2 * ROPE_FREQS)) < ROPE_FREQS
        for c in range(tn // HEAD_DIM):
            x = acc[:, c * HEAD_DIM:(c + 1) * HEAD_DIM]
            partner = jnp.where(first_half,
                                pltpu.roll(x, HEAD_DIM - ROPE_FREQS, 1),
                                pltpu.roll(x, ROPE_FREQS, 1))
            o_ref[:, c * HEAD_DIM:(c + 1) * HEAD_DIM] = ((x * cos + partner * sin) * scale).astype(BF16)

    @pl.when(is_natq)
    def _():
        o_ref[...] = (acc * ATTN_SCALE).astype(BF16)

    @pl.when(jnp.logical_not(jnp.logical_or(is_rope, is_natq)))
    def _():
        o_ref[...] = acc.astype(BF16)


def _inproj(dm, a, w_all, layer, cos_t, sin_t):
    tm, tn = dm.tm, dm.tn
    lat_tiles = dm.n_lat // tm
    per_seq = dm.S // tm

    def tab_map(i, j):
        return (jnp.where(i < lat_tiles, i % per_seq, per_seq), 0)

    kern = functools.partial(_inproj_kernel, tn=tn, n_rope_q=dm.swa_q // tn, n_rope=dm.off_swa_v // tn,
                             nat_q0=dm.off_nat_q // tn, nat_q1=dm.off_nat_k // tn)
    return pl.pallas_call(
        kern,
        grid=(dm.rows // tm, dm.in_cols // tn),
        in_specs=[pl.BlockSpec((tm, dm.D), lambda i, j: (i, 0)),
                  pl.BlockSpec((None, dm.D, tn), lambda i, j: (layer, 0, j)),
                  pl.BlockSpec((tm, HEAD_DIM), tab_map),
                  pl.BlockSpec((tm, HEAD_DIM), tab_map)],
        out_specs=pl.BlockSpec((tm, tn), lambda i, j: (i, j)),
        out_shape=jax.ShapeDtypeStruct((dm.rows, dm.in_cols), BF16),
        compiler_params=_params(2),
        name="in_proj",
    )(a, w_all, cos_t, sin_t)


def _mm_kernel(*refs, k_splits):
    a_refs, w_ref, o_ref = refs[:-2], refs[-2], refs[-1]
    acc = None
    k0 = 0
    for a_ref, kw in zip(a_refs, k_splits):
        part = jnp.dot(a_ref[...], w_ref[k0:k0 + kw, :], preferred_element_type=F32)
        acc = part if acc is None else acc + part
        k0 += kw
    o_ref[...] = acc.astype(o_ref.dtype)


def _matmul(a_list, w_all, layer, rows, tm, tn, out_dtype, name):
    _, K, N = w_all.shape
    k_splits = tuple(a.shape[1] for a in a_list)
    assert sum(k_splits) == K
    in_specs = [pl.BlockSpec((tm, kw), lambda i, j: (i, 0)) for kw in k_splits]
    in_specs.append(pl.BlockSpec((None, K, tn), lambda i, j: (layer, 0, j)))
    return pl.pallas_call(
        functools.partial(_mm_kernel, k_splits=k_splits),
        grid=(rows // tm, N // tn),
        in_specs=in_specs,
        out_specs=pl.BlockSpec((tm, tn), lambda i, j: (i, j)),
        out_shape=jax.ShapeDtypeStruct((rows, N), out_dtype),
        compiler_params=_params(2),
        name=name,
    )(*a_list, w_all)


def _dot_nt(a, b):
    return lax.dot_general(a, b, (((1,), (1,)), ((), ())), preferred_element_type=F32)


def _rms_gain(o, g):
    ms = jnp.mean(o * o, axis=-1, keepdims=True)
    return (o * lax.rsqrt(ms + RMS_EPS) * g).astype(BF16)


def _swa_mask_table():
    qi = np.arange(SWA_BLOCK)[:, None]
    kj = np.arange(SWA_BLOCK)[None, :]
    prev_ok = kj >= qi
    cur_ok = np.ones((SWA_BLOCK, SWA_BLOCK), bool)
    next_ok = kj <= qi
    none = np.zeros((SWA_BLOCK, SWA_BLOCK), bool)
    kinds = [np.concatenate([none, cur_ok, next_ok], 1),
             np.concatenate([prev_ok, cur_ok, next_ok], 1),
             np.concatenate([prev_ok, cur_ok, none], 1),
             np.concatenate([none, none, none], 1)]
    return jnp.asarray(np.where(np.stack(kinds), 0.0, NEG_INF).astype(np.float32))


def _swa_kernel(sink_ref, q_ref, kp_ref, kc_ref, kn_ref, vp_ref, vc_ref, vn_ref, kx_ref, vx_ref,
                mask_ref, g_ref, o_ref, acc_ref, *, kv_heads):
    mask = mask_ref[0]
    for hk in range(kv_heads):
        ks = slice(hk * HEAD_DIM, (hk + 1) * HEAD_DIM)
        q = jnp.concatenate([q_ref[:, (hk * SWA_GROUP + g) * HEAD_DIM:(hk * SWA_GROUP + g + 1) * HEAD_DIM]
                             for g in range(SWA_GROUP)], axis=0)
        kb = jnp.concatenate([kp_ref[:, ks], kc_ref[:, ks], kn_ref[:, ks]], axis=0)
        vb = jnp.concatenate([vp_ref[:, ks], vc_ref[:, ks], vn_ref[:, ks]], axis=0)
        s = _dot_nt(q, kb)
        s = (s.reshape(SWA_GROUP, SWA_BLOCK, 3 * SWA_BLOCK) + mask[None]).reshape(SWA_GROUP * SWA_BLOCK, 3 * SWA_BLOCK)
        sx = _dot_nt(q, kx_ref[:, ks])
        sink = jnp.concatenate([jnp.full((SWA_BLOCK, 1), sink_ref[hk * SWA_GROUP + g], F32)
                                for g in range(SWA_GROUP)], axis=0)
        m = jnp.maximum(jnp.maximum(jnp.max(s, axis=-1, keepdims=True), jnp.max(sx, axis=-1, keepdims=True)), sink)
        p = jnp.exp(s - m)
        px = jnp.exp(sx - m)
        denom = jnp.sum(p, axis=-1, keepdims=True) + jnp.sum(px, axis=-1, keepdims=True) + jnp.exp(sink - m)
        o = jnp.dot(p.astype(BF16), vb, preferred_element_type=F32)
        o += jnp.dot(px.astype(BF16), vx_ref[:, ks], preferred_element_type=F32)
        o = o / denom
        for g in range(SWA_GROUP):
            h = hk * SWA_GROUP + g
            acc_ref[:, h * HEAD_DIM:(h + 1) * HEAD_DIM] = o[g * SWA_BLOCK:(g + 1) * SWA_BLOCK]
    o_ref[...] = _rms_gain(acc_ref[...], g_ref[...])


def _swa(dm, u, sink, mask_tab, gain):
    nb = dm.S // SWA_BLOCK
    lat_blocks = dm.B * nb
    ctx_per = dm.L // SWA_BLOCK
    n_blocks = lat_blocks + dm.B * ctx_per
    kvw = dm.swa_kv
    k_col = dm.off_swa_k // kvw
    v_col = dm.off_swa_v // kvw
    ctx_row0 = dm.n_lat // dm.L

    def batch_of(g):
        return jnp.where(g < lat_blocks, g // nb, (g - lat_blocks) // ctx_per)

    def nbr(g, d):
        n = g % nb
        return jnp.where(g < lat_blocks, (g // nb) * nb + jnp.clip(n + d, 0, nb - 1), g)

    def kind(g):
        n = g % nb
        return jnp.where(g < lat_blocks, jnp.where(n == 0, 0, jnp.where(n == nb - 1, 2, 1)), 3)

    blk = lambda col, d: pl.BlockSpec((SWA_BLOCK, kvw), lambda g: (nbr(g, d), col))
    ctx = lambda col: pl.BlockSpec((dm.L, kvw), lambda g: (ctx_row0 + batch_of(g), col))
    in_specs = [pl.BlockSpec(memory_space=pltpu.SMEM),
                pl.BlockSpec((SWA_BLOCK, dm.swa_q), lambda g: (g, 0)),
                blk(k_col, -1), blk(k_col, 0), blk(k_col, 1),
                blk(v_col, -1), blk(v_col, 0), blk(v_col, 1),
                ctx(k_col), ctx(v_col),
                pl.BlockSpec((1, SWA_BLOCK, 3 * SWA_BLOCK), lambda g: (kind(g), 0, 0)),
                pl.BlockSpec((1, dm.swa_q), lambda g: (0, 0))]
    return pl.pallas_call(
        functools.partial(_swa_kernel, kv_heads=dm.swa_kv_heads),
        grid=(n_blocks,),
        in_specs=in_specs,
        out_specs=pl.BlockSpec((SWA_BLOCK, dm.swa_q), lambda g: (g, 0)),
        out_shape=jax.ShapeDtypeStruct((dm.rows, dm.swa_q), BF16),
        scratch_shapes=[pltpu.VMEM((SWA_BLOCK, dm.swa_q), F32)],
        compiler_params=_params(1),
        name="window_attn",
    )(sink, u, u, u, u, u, u, u, u, u, mask_tab, gain)


def _nat_bias_tables(dm, rpb_all):
    nq, nk = NAT_QROWS, 3 * NAT_QROWS
    i = np.arange(nq)[:, None]
    j = np.arange(nk)[None, :]
    row_ok = [(j >= nq) & (j < nq + NAT_ROWS) & (i >= 0),
              (j - i >= 0) & (j - i < NAT_ROWS),
              (j >= 0) & (j < NAT_ROWS) & (i >= 0)]
    dr = np.clip(j - i + (NAT_ROWS - 1) - nq, 0, 2 * NAT_ROWS - 2)
    t_row = np.zeros((3, 2 * NAT_ROWS - 1, nq, nk), np.float32)
    for k in range(3):
        for a in range(nq):
            for b in range(nk):
                if row_ok[k][a, b]:
                    t_row[k, dr[a, b], a, b] = 1.0
    cq = np.arange(GRID_W)[:, None]
    kc = np.arange(GRID_W)[None, :]
    cs = np.clip(cq - NAT_COLS // 2, 0, GRID_W - NAT_COLS)
    col_ok = (kc >= cs) & (kc < cs + NAT_COLS)
    dc = np.clip(kc - cq + NAT_COLS - 1, 0, 2 * NAT_COLS - 2)
    t_col = np.zeros((2 * NAT_COLS - 1, GRID_W, GRID_W), np.float32)
    for a in range(GRID_W):
        for b in range(GRID_W):
            if col_ok[a, b]:
                t_col[dc[a, b], a, b] = 1.0
    t_row, t_col = jnp.asarray(t_row), jnp.asarray(t_col)
    val = jnp.einsum('lhrc,trij,cqk->lthiqjk', rpb_all, t_row, t_col, precision=lax.Precision.HIGHEST)
    ok = jnp.einsum('trij,cqk->tiqjk', t_row, t_col) > 0.5
    bias = jnp.where(ok[None, :, None], val, NEG_INF)
    depth, _, H = bias.shape[:3]
    bias = bias.reshape(depth, 3, H, NAT_QBLOCK, 3 * NAT_QBLOCK)
    none = jnp.full((depth, 1, H, NAT_QBLOCK, 3 * NAT_QBLOCK), NEG_INF, F32)
    return jnp.concatenate([bias, none], axis=1)


def _nat_kernel(q_ref, kp_ref, kc_ref, kn_ref, vp_ref, vc_ref, vn_ref, kx_ref, vx_ref,
                bias_ref, g_ref, o_ref, acc_ref, *, heads):
    for h in range(heads):
        hs = slice(h * HEAD_DIM, (h + 1) * HEAD_DIM)
        q = q_ref[:, hs]
        kb = jnp.concatenate([kp_ref[:, hs], kc_ref[:, hs], kn_ref[:, hs]], axis=0)
        vb = jnp.concatenate([vp_ref[:, hs], vc_ref[:, hs], vn_ref[:, hs]], axis=0)
        s = _dot_nt(q, kb) + bias_ref[0, h]
        sx = _dot_nt(q, kx_ref[:, hs])
        m = jnp.maximum(jnp.max(s, axis=-1, keepdims=True), jnp.max(sx, axis=-1, keepdims=True))
        p = jnp.exp(s - m)
        px = jnp.exp(sx - m)
        denom = jnp.sum(p, axis=-1, keepdims=True) + jnp.sum(px, axis=-1, keepdims=True)
        o = jnp.dot(p.astype(BF16), vb, preferred_element_type=F32)
        o += jnp.dot(px.astype(BF16), vx_ref[:, hs], preferred_element_type=F32)
        acc_ref[:, hs] = o / denom
    o_ref[...] = _rms_gain(acc_ref[...], g_ref[...])


def _nat(dm, u, bias_tab, gain):
    nb = dm.S // NAT_QBLOCK
    lat_blocks = dm.B * nb
    ctx_per = dm.L // NAT_QBLOCK
    n_blocks = lat_blocks + dm.B * ctx_per
    w = dm.nat_w
    q_col, k_col, v_col = dm.off_nat_q // w, dm.off_nat_k // w, dm.off_nat_v // w
    ctx_row0 = dm.n_lat // dm.L

    def batch_of(g):
        return jnp.where(g < lat_blocks, g // nb, (g - lat_blocks) // ctx_per)

    def nbr(g, d):
        n = g % nb
        return jnp.where(g < lat_blocks, (g // nb) * nb + jnp.clip(n + d, 0, nb - 1), g)

    def kind(g):
        n = g % nb
        return jnp.where(g < lat_blocks, jnp.where(n == 0, 0, jnp.where(n == nb - 1, 2, 1)), 3)

    blk = lambda col, d: pl.BlockSpec((NAT_QBLOCK, w), lambda g: (nbr(g, d), col))
    ctx = lambda col: pl.BlockSpec((dm.L, w), lambda g: (ctx_row0 + batch_of(g), col))
    in_specs = [pl.BlockSpec((NAT_QBLOCK, w), lambda g: (g, q_col)),
                blk(k_col, -1), blk(k_col, 0), blk(k_col, 1),
                blk(v_col, -1), blk(v_col, 0), blk(v_col, 1),
                ctx(k_col), ctx(v_col),
                pl.BlockSpec((1, dm.nat_heads, NAT_QBLOCK, 3 * NAT_QBLOCK), lambda g: (kind(g), 0, 0, 0)),
                pl.BlockSpec((1, w), lambda g: (0, 0))]
    return pl.pallas_call(
        functools.partial(_nat_kernel, heads=dm.nat_heads),
        grid=(n_blocks,),
        in_specs=in_specs,
        out_specs=pl.BlockSpec((NAT_QBLOCK, w), lambda g: (g, 0)),
        out_shape=jax.ShapeDtypeStruct((dm.rows, w), BF16),
        scratch_shapes=[pltpu.VMEM((NAT_QBLOCK, w), F32)],
        compiler_params=_params(1),
        name="nbr_attn",
    )(u, u, u, u, u, u, u, u, u, bias_tab, gain)


CONV_TILE = 256
HALO = 16


def _conv_kernel(u_ref, prev_ref, next_ref, w_ref, b_ref, g_ref, o_ref, *, cw, tiles_per_seq, lat_tiles):
    i = pl.program_id(0)
    is_ctx = i >= lat_tiles
    n = i % tiles_per_seq
    has_prev = jnp.logical_and(jnp.logical_not(is_ctx), n != 0)
    has_next = jnp.logical_and(jnp.logical_not(is_ctx), n != tiles_per_seq - 1)
    x = u_ref[:, 0:cw].astype(F32)
    bg = u_ref[:, cw:2 * cw].astype(F32)
    cg = u_ref[:, 2 * cw:3 * cw].astype(F32)
    z = cg * x
    zp = (prev_ref[HALO - 1:HALO, 2 * cw:3 * cw].astype(F32) * prev_ref[HALO - 1:HALO, 0:cw].astype(F32))
    zn = (next_ref[0:1, 2 * cw:3 * cw].astype(F32) * next_ref[0:1, 0:cw].astype(F32))
    zp = jnp.where(has_prev, zp, 0.0)
    zn = jnp.where(has_next, zn, 0.0)
    row = lax.broadcasted_iota(jnp.int32, z.shape, 0)
    z_m1 = jnp.where(row == 0, zp, pltpu.roll(z, 1, 0))
    z_p1 = jnp.where(row == CONV_TILE - 1, zn, pltpu.roll(z, CONV_TILE - 1, 0))
    w = w_ref[...]
    conv = b_ref[...] + z_m1 * w[0:1] + z * w[1:2] + z_p1 * w[2:3]
    o_ref[...] = _rms_gain(bg * conv, g_ref[...])


def _conv(dm, u, w, b, gain):
    assert dm.L == CONV_TILE, "context sequences are one convolution tile"
    cw = dm.conv_w
    tiles = dm.rows // CONV_TILE
    lat_tiles = dm.n_lat // CONV_TILE
    col = dm.off_conv // (3 * cw)
    per_halo = CONV_TILE // HALO
    last_halo = dm.rows // HALO - 1
    kern = functools.partial(_conv_kernel, cw=cw, tiles_per_seq=dm.S // CONV_TILE, lat_tiles=lat_tiles)
    return pl.pallas_call(
        kern,
        grid=(tiles,),
        in_specs=[pl.BlockSpec((CONV_TILE, 3 * cw), lambda i: (i, col)),
                  pl.BlockSpec((HALO, 3 * cw), lambda i: (jnp.maximum(i * per_halo - 1, 0), col)),
                  pl.BlockSpec((HALO, 3 * cw), lambda i: (jnp.minimum((i + 1) * per_halo, last_halo), col)),
                  pl.BlockSpec((3, cw), lambda i: (0, 0)),
                  pl.BlockSpec((1, cw), lambda i: (0, 0)),
                  pl.BlockSpec((1, cw), lambda i: (0, 0))],
        out_specs=pl.BlockSpec((CONV_TILE, cw), lambda i: (i, 0)),
        out_shape=jax.ShapeDtypeStruct((dm.rows, cw), BF16),
        compiler_params=_params(1),
        name="gated_conv",
    )(u, u, u, w, b, gain)


def _layer_norm(y, g, b):
    mu = jnp.mean(y, axis=-1, keepdims=True)
    yc = y - mu
    var = jnp.mean(yc * yc, axis=-1, keepdims=True)
    return yc * lax.rsqrt(var + LN_EPS) * g + b


def _route(logits):
    lane = lax.broadcasted_iota(jnp.int32, logits.shape, 1).astype(F32)
    big = float(ROUTER_LANES)
    is_g = lane < N_GROUPS
    gl = jnp.where(is_g, logits, NEG_INF)
    gmax = jnp.max(gl, axis=-1, keepdims=True)
    ge = jnp.where(is_g, jnp.exp(gl - gmax), 0.0)
    gprob = ge / jnp.sum(ge, axis=-1, keepdims=True)
    g_p = jnp.max(gprob, axis=-1, keepdims=True)
    g_idx = jnp.min(jnp.where(jnp.logical_and(is_g, gprob == g_p), lane, big), axis=-1, keepdims=True)
    lo = N_GROUPS + g_idx * EXPERTS_PER_GROUP
    in_grp = jnp.logical_and(lane >= lo, lane < lo + EXPERTS_PER_GROUP)
    el = jnp.where(in_grp, logits, NEG_INF)
    v1 = jnp.max(el, axis=-1, keepdims=True)
    i1 = jnp.min(jnp.where(jnp.logical_and(in_grp, el == v1), lane, big), axis=-1, keepdims=True)
    el2 = jnp.where(lane == i1, NEG_INF, el)
    v2 = jnp.max(el2, axis=-1, keepdims=True)
    rest = jnp.logical_and(in_grp, lane != i1)
    i2 = jnp.min(jnp.where(jnp.logical_and(rest, el2 == v2), lane, big), axis=-1, keepdims=True)
    e2 = jnp.exp(v2 - v1)
    den = 1.0 + e2
    w1 = g_p / den
    w2 = g_p * (e2 / den)
    gates = jnp.where(lane == i1, w1, jnp.where(lane == i2, w2, 0.0))
    return jnp.where(lane == ROUTE_GROUP_LANE, g_idx, gates)


HIGH_HALF = 0xFFFF0000


def _pack_bf16_pairs(x):
    w = x.shape[1] // 2
    hi = lax.bitcast_convert_type(x[:, :w].astype(BF16).astype(F32), jnp.uint32)
    lo = lax.bitcast_convert_type(x[:, w:].astype(BF16).astype(F32), jnp.uint32)
    return hi | (lo >> 16)


def _unpack_bf16_pairs(p):
    hi = lax.bitcast_convert_type(p & jnp.uint32(HIGH_HALF), F32)
    lo = lax.bitcast_convert_type(p << 16, F32)
    return hi, lo


def _start_row_gather(idx_ref, idx0, src_hbm, dst_ref, sem, row0, n_rows):
    def issue(k, carry):
        j = row0 + k
        row = idx_ref[idx0 + j]
        pltpu.make_async_copy(src_hbm.at[pl.ds(row, 1)], dst_ref.at[pl.ds(j, 1)], sem).start()
        return carry

    lax.fori_loop(0, n_rows, issue, 0, unroll=8)


def _wait_row_gather(src_hbm, dst_ref, sem):
    pltpu.make_async_copy(src_hbm.at[pl.ds(0, dst_ref.shape[0])], dst_ref, sem).wait()


def _ln1_kernel(h_ref, mix_ref, mod_ref, g_ref, b_ref, wr_ref, br_ref, h_out, t_out, route_out, *, D, alpha):
    mod = mod_ref[0]
    y = alpha * h_ref[...] + mod[:, 2 * D:3 * D] * mix_ref[...]
    hn = _layer_norm(y, g_ref[...], b_ref[...])
    h_out[...] = hn
    t = hn * (1.0 + mod[:, 4 * D:5 * D]) + mod[:, 3 * D:4 * D]
    logits = jnp.dot(t, wr_ref[...], preferred_element_type=F32, precision=lax.Precision.HIGHEST) + br_ref[...]
    route = _route(logits)
    route_out[...] = route
    t_out[:, 0:D // 2] = _pack_bf16_pairs(t)
    t_out[:, D // 2:] = lax.bitcast_convert_type(route, jnp.uint32)


def _ln1(dm, rows, h, mix, mod, g, b, wr, br):
    tr, D = dm.tr, dm.D
    row = lambda width: pl.BlockSpec((tr, width), lambda i: (i, 0))
    vec = lambda width: pl.BlockSpec((1, width), lambda i: (0, 0))
    return pl.pallas_call(
        functools.partial(_ln1_kernel, D=D, alpha=dm.alpha),
        grid=(rows // tr,),
        in_specs=[row(D), row(D), _mod_spec(dm, tr), vec(D), vec(D),
                  pl.BlockSpec((D, ROUTER_LANES), lambda i: (0, 0)), vec(ROUTER_LANES)],
        out_specs=[row(D), row(dm.packed_w), row(ROUTER_LANES)],
        out_shape=[jax.ShapeDtypeStruct((rows, D), F32), jax.ShapeDtypeStruct((rows, dm.packed_w), jnp.uint32),
                   jax.ShapeDtypeStruct((rows, ROUTER_LANES), F32)],
        compiler_params=_params(1),
        name="mix_residual_norm_route",
    )(h, mix, mod, g, b, wr, br)


def _ln2_kernel(pos_ref, h_ref, f_hbm, mod_ref, modn_ref, g_ref, b_ref, h_out, a_out, fbuf, sem, *,
                D, alpha, tr, n_tiles):
    i = pl.program_id(0)
    slot = i % 2

    @pl.when(i == 0)
    def _():
        _start_row_gather(pos_ref, 0, f_hbm, fbuf.at[0], sem.at[0], 0, tr)

    @pl.when(i + 1 < n_tiles)
    def _():
        _start_row_gather(pos_ref, (i + 1) * tr, f_hbm, fbuf.at[1 - slot], sem.at[1 - slot], 0, tr)

    _wait_row_gather(f_hbm, fbuf.at[slot], sem.at[slot])
    mod = mod_ref[0]
    f_hi, f_lo = _unpack_bf16_pairs(fbuf[slot])
    ffn = jnp.concatenate([f_hi, f_lo], axis=1)
    y = alpha * h_ref[...] + mod[:, 5 * D:6 * D] * ffn
    hn = _layer_norm(y, g_ref[...], b_ref[...])
    h_out[...] = hn
    if a_out is not None:
        modn = modn_ref[0]
        a_out[...] = (hn * (1.0 + modn[:, D:2 * D]) + modn[:, 0:D]).astype(BF16)


def _ln2_last_kernel(pos_ref, h_ref, f_hbm, mod_ref, g_ref, b_ref, h_out, fbuf, sem, **kw):
    _ln2_kernel(pos_ref, h_ref, f_hbm, mod_ref, None, g_ref, b_ref, h_out, None, fbuf, sem, **kw)


def _ln2(dm, rows, pos, h, ffn_sorted, mod, mod_next, g, b):
    tr, D = dm.tr, dm.D
    n_tiles = rows // tr
    row = pl.BlockSpec((tr, D), lambda i, p: (i, 0))
    vec = pl.BlockSpec((1, D), lambda i, p: (0, 0))
    hbm = pl.BlockSpec(memory_space=pl.ANY)
    scratch = [pltpu.VMEM((2, tr, D // 2), jnp.uint32), pltpu.SemaphoreType.DMA((2,))]
    kw = dict(D=D, alpha=dm.alpha, tr=tr, n_tiles=n_tiles)
    if mod_next is None:
        return pl.pallas_call(
            functools.partial(_ln2_last_kernel, **kw),
            grid_spec=pltpu.PrefetchScalarGridSpec(
                num_scalar_prefetch=1, grid=(n_tiles,),
                in_specs=[row, hbm, _mod_spec(dm, tr), vec, vec],
                out_specs=row, scratch_shapes=scratch),
            out_shape=jax.ShapeDtypeStruct((rows, D), F32),
            compiler_params=_params(1),
            name="ffn_residual_norm_last",
        )(pos, h, ffn_sorted, mod, g, b), None
    return pl.pallas_call(
        functools.partial(_ln2_kernel, **kw),
        grid_spec=pltpu.PrefetchScalarGridSpec(
            num_scalar_prefetch=1, grid=(n_tiles,),
            in_specs=[row, hbm, _mod_spec(dm, tr), _mod_spec(dm, tr), vec, vec],
            out_specs=[row, row], scratch_shapes=scratch),
        out_shape=[jax.ShapeDtypeStruct((rows, D), F32), jax.ShapeDtypeStruct((rows, D), BF16)],
        compiler_params=_params(1),
        name="ffn_residual_norm",
    )(pos, h, ffn_sorted, mod, mod_next, g, b)


def _dispatch_plan(rows, route):
    tm = EXPERT_TILE
    n_tiles = rows // tm + N_GROUPS
    i32 = jnp.int32
    gidx = route[:, ROUTE_GROUP_LANE].astype(i32)
    onehot = (gidx[:, None] == jnp.arange(N_GROUPS, dtype=i32)[None]).astype(i32)
    csum = jnp.cumsum(onehot, axis=0)
    counts = csum[-1]
    rank = jnp.sum(onehot * csum, axis=1) - 1
    padded = (counts + tm - 1) // tm * tm
    pstart = jnp.cumsum(padded) - padded
    ustart = jnp.cumsum(counts) - counts
    total = jnp.sum(padded)
    pos = jnp.sum(onehot * pstart[None], axis=1) + rank
    order = jnp.argsort(gidx, stable=True).astype(i32)
    slot = jnp.arange(n_tiles * tm, dtype=i32)
    sgrp = jnp.sum((slot[:, None] >= pstart[None, 1:]).astype(i32), axis=1)
    shot = (sgrp[:, None] == jnp.arange(N_GROUPS, dtype=i32)[None]).astype(i32)
    k = slot - jnp.sum(shot * pstart[None], axis=1)
    valid = jnp.logical_and(k < jnp.sum(shot * counts[None], axis=1), slot < total)
    sorted_at = jnp.clip(jnp.sum(shot * ustart[None], axis=1) + k, 0, rows - 1)
    src = jnp.where(valid, order[sorted_at], slot % rows)
    tile_slot = jnp.arange(n_tiles, dtype=i32) * tm
    tgrp = jnp.where(tile_slot < total, sgrp[::tm], N_GROUPS)
    return src, pos.astype(i32), tgrp.astype(i32)


def _experts_kernel(src_ref, tgrp_ref, t_hbm, wg_ref, wu_ref, wd_ref, o_ref, tbuf, sem, tb_ref, gate_ref, acc_ref,
                    *, D, tm, n_tiles):
    i = pl.program_id(0)
    e = pl.program_id(1)
    slot = i % 2
    grp = tgrp_ref[i]
    active = grp < N_GROUPS
    half = D // 2
    part_rows = tm // EXPERTS_PER_GROUP

    @pl.when(jnp.logical_and(i == 0, e == 0))
    def _():
        _start_row_gather(src_ref, 0, t_hbm, tbuf.at[0], sem.at[0], 0, tm)

    @pl.when(e == 0)
    def _():
        _wait_row_gather(t_hbm, tbuf.at[slot], sem.at[slot])
        hi, lo = _unpack_bf16_pairs(tbuf[slot, :, 0:half])
        tb_ref[:, 0:half] = hi.astype(BF16)
        tb_ref[:, half:] = lo.astype(BF16)
        gate_ref[...] = lax.bitcast_convert_type(tbuf[slot, :, half:], F32)

    @pl.when(i + 1 < n_tiles)
    def _():
        _start_row_gather(src_ref, (i + 1) * tm, t_hbm, tbuf.at[1 - slot], sem.at[1 - slot],
                          e * part_rows, part_rows)

    @pl.when(active)
    def _():
        t = tb_ref[...]
        a = jnp.dot(t, wg_ref[...], preferred_element_type=F32)
        b = jnp.dot(t, wu_ref[...], preferred_element_type=F32)
        gates = gate_ref[...]
        lane = lax.broadcasted_iota(jnp.int32, gates.shape, 1)
        gcol = jnp.sum(jnp.where(lane == N_GROUPS + grp * EXPERTS_PER_GROUP + e, gates, 0.0),
                       axis=-1, keepdims=True)
        hid = (a * jax.nn.sigmoid(a) * b * gcol).astype(BF16)
        part = jnp.dot(hid, wd_ref[...], preferred_element_type=F32)

        @pl.when(e == 0)
        def _():
            acc_ref[...] = part

        @pl.when(e > 0)
        def _():
            acc_ref[...] += part

    last = e == EXPERTS_PER_GROUP - 1

    @pl.when(jnp.logical_and(last, active))
    def _():
        o_ref[...] = _pack_bf16_pairs(acc_ref[...])

    @pl.when(jnp.logical_and(last, jnp.logical_not(active)))
    def _():
        o_ref[...] = jnp.zeros(o_ref.shape, o_ref.dtype)


def _experts(dm, layer, src, tgrp, t_packed, wg_all, wu_all, wd_all):
    tm, D, F = EXPERT_TILE, dm.D, dm.d_expert
    n_tiles = tgrp.shape[0]

    def w_map(i, e, sr, tg):
        return (layer, jnp.minimum(tg[i], N_GROUPS - 1) * EXPERTS_PER_GROUP + e, 0, 0)

    return pl.pallas_call(
        functools.partial(_experts_kernel, D=D, tm=tm, n_tiles=n_tiles),
        grid_spec=pltpu.PrefetchScalarGridSpec(
            num_scalar_prefetch=2,
            grid=(n_tiles, EXPERTS_PER_GROUP),
            in_specs=[pl.BlockSpec(memory_space=pl.ANY),
                      pl.BlockSpec((None, None, D, F), w_map),
                      pl.BlockSpec((None, None, D, F), w_map),
                      pl.BlockSpec((None, None, F, D), w_map)],
            out_specs=pl.BlockSpec((tm, D // 2), lambda i, e, sr, tg: (i, 0)),
            scratch_shapes=[pltpu.VMEM((2, tm, dm.packed_w), jnp.uint32), pltpu.SemaphoreType.DMA((2,)),
                            pltpu.VMEM((tm, D), BF16), pltpu.VMEM((tm, ROUTER_LANES), F32),
                            pltpu.VMEM((tm, D), F32)]),
        out_shape=jax.ShapeDtypeStruct((n_tiles * tm, D // 2), jnp.uint32),
        compiler_params=_params(2),
        name="experts",
    )(src, tgrp, t_packed, wg_all, wu_all, wd_all)


def kernel(x, c, ctx, c_ctx, w_ada, b_ada, w_in, conv_w, conv_b, attn_sink, nat_rpb, mix_norm_g, w_out,
           ln1_g, ln1_b, w_router_group, b_router_group, w_router_expert, b_router_expert,
           w_gate, w_up, w_down, ln2_g, ln2_b):
    B, S, D = x.shape
    L = ctx.shape[1]
    depth = w_in.shape[0]
    dm = Dims(B, S, L, D, depth, w_gate.shape[-1])

    xin = jnp.concatenate([c, c_ctx[None], jnp.zeros((MOD_ROWS - B - 1, D), F32)], axis=0)
    mods = _ada_all(xin, w_ada, b_ada).reshape(depth, MOD_ROWS, 1, 6 * D)

    cos_t, sin_t = _rope_tables(dm)
    swa_mask = _swa_mask_table()
    nat_bias = _nat_bias_tables(dm, nat_rpb)

    w_in_b = w_in.astype(BF16)
    w_out_b = w_out.astype(BF16)
    w_gate_b = w_gate.astype(BF16)
    w_up_b = w_up.astype(BF16)
    w_down_b = w_down.astype(BF16)
    pad = jnp.zeros((depth, D, ROUTER_LANES - N_GROUPS - N_EXPERTS), F32)
    w_route = jnp.concatenate([w_router_group, w_router_expert, pad], axis=-1)
    b_route = jnp.concatenate([b_router_group, b_router_expert, pad[:, 0]], axis=-1).reshape(depth, 1, ROUTER_LANES)

    h = jnp.concatenate([x.reshape(B * S, D), ctx.reshape(B * L, D)], axis=0)
    a = _modulate(dm, h, mods[0])
    for i in range(depth):
        last = i == depth - 1
        rows = dm.n_lat if last else dm.rows
        u = _inproj(dm, a, w_in_b, i, cos_t, sin_t)
        gain = mix_norm_g[i].reshape(1, D)
        y_swa = _swa(dm, u, attn_sink[i], swa_mask, gain[:, :dm.swa_q])
        y_conv = _conv(dm, u, conv_w[i], conv_b[i].reshape(1, -1), gain[:, dm.swa_q:dm.swa_q + dm.conv_w])
        y_nat = _nat(dm, u, nat_bias[i], gain[:, dm.swa_q + dm.conv_w:])
        mix = _matmul([y_swa, y_conv, y_nat], w_out_b, i, rows, dm.tm, 512, F32, "out_proj")
        h_mid, t_packed, route = _ln1(dm, rows, h, mix, mods[i], ln1_g[i].reshape(1, D), ln1_b[i].reshape(1, D),
                                      w_route[i], b_route[i])
        src, pos, tgrp = _dispatch_plan(rows, route)
        ffn_sorted = _experts(dm, i, src, tgrp, t_packed, w_gate_b, w_up_b, w_down_b)
        h, a = _ln2(dm, rows, pos, h_mid, ffn_sorted, mods[i], None if last else mods[i + 1],
                    ln2_g[i].reshape(1, D), ln2_b[i].reshape(1, D))
    return h.reshape(B, S, D)
```

```python
import functools

import numpy as np
import jax
import jax.numpy as jnp
from jax import lax
from jax.experimental import pallas as pl
from jax.experimental.pallas import tpu as pltpu

F32 = jnp.float32
BF16 = jnp.bfloat16

HEAD_DIM = 128
LANES = 128
GRID_W = 64
ROPE_THETA = 10000.0
ROPE_FREQS = HEAD_DIM // 4
SWA_GROUP = 4
SWA_BLOCK = 128
NAT_ROWS = 8
NAT_COLS = 16
NAT_QROWS = 4
NAT_QBLOCK = NAT_QROWS * GRID_W
N_GROUPS = 4
EXPERTS_PER_GROUP = 4
N_EXPERTS = N_GROUPS * EXPERTS_PER_GROUP
ROUTER_LANES = 128
ROUTE_GROUP_LANE = 0
EXPERT_TILE = 512
LN_EPS = 1e-5
RMS_EPS = 1e-6
NEG_INF = -1e30
ATTN_SCALE = HEAD_DIM ** -0.5
MOD_ROWS = 8
VMEM_LIMIT = 56 * 1024 * 1024


def _params(n_axes):
    return pltpu.CompilerParams(dimension_semantics=("arbitrary",) * n_axes,
                                vmem_limit_bytes=VMEM_LIMIT)


class Dims:
    def __init__(self, B, S, L, D, depth, d_expert):
        self.B, self.S, self.L, self.D, self.depth, self.d_expert = B, S, L, D, depth, d_expert
        self.swa_q = D // 2
        self.swa_heads = self.swa_q // HEAD_DIM
        self.swa_kv_heads = self.swa_heads // SWA_GROUP
        self.swa_kv = self.swa_kv_heads * HEAD_DIM
        self.conv_w = D // 4
        self.nat_w = D // 4
        self.nat_heads = self.nat_w // HEAD_DIM
        self.off_swa_k = self.swa_q
        self.off_swa_v = self.off_swa_k + self.swa_kv
        self.off_conv = self.off_swa_v + self.swa_kv
        self.off_nat_q = self.off_conv + 3 * self.conv_w
        self.off_nat_k = self.off_nat_q + self.nat_w
        self.off_nat_v = self.off_nat_k + self.nat_w
        self.in_cols = self.off_nat_v + self.nat_w
        self.n_lat = B * S
        self.n_ctx = B * L
        self.rows = self.n_lat + self.n_ctx
        self.grid_rows = S // GRID_W
        self.packed_w = D // 2 + ROUTER_LANES
        assert self.rows % EXPERT_TILE == 0 and self.n_lat % EXPERT_TILE == 0
        self.alpha = (2.0 * depth) ** 0.25
        self.tm = 1024 if (S % 1024 == 0 and self.n_ctx % 1024 == 0) else 256
        self.tr = 256
        self.tn = self.swa_kv
        assert S % self.tm == 0 and self.n_ctx % self.tm == 0
        assert S % NAT_QBLOCK == 0 and L % NAT_QBLOCK == 0 and self.grid_rows >= 3 * NAT_QROWS
        assert S // SWA_BLOCK >= 3 and L % SWA_BLOCK == 0
        assert self.off_conv % (3 * self.conv_w) == 0
        assert B + 1 <= MOD_ROWS


def _ada_kernel(x_ref, w_ref, b_ref, o_ref):
    x = x_ref[...]
    act = x * jax.nn.sigmoid(x)
    a_hi = act.astype(BF16)
    a_lo = (act - a_hi.astype(F32)).astype(BF16)
    w = w_ref[0]
    w_hi = w.astype(BF16)
    w_lo = (w - w_hi.astype(F32)).astype(BF16)
    acc = jnp.dot(a_hi, w_hi, preferred_element_type=F32)
    acc += jnp.dot(a_lo, w_hi, preferred_element_type=F32)
    acc += jnp.dot(a_hi, w_lo, preferred_element_type=F32)
    o_ref[0] = acc + b_ref[0]


def _ada_all(xin, w_ada, b_ada):
    depth, D, n6 = w_ada.shape
    tn = 512
    return pl.pallas_call(
        _ada_kernel,
        grid=(depth, n6 // tn),
        in_specs=[pl.BlockSpec((MOD_ROWS, D), lambda l, j: (0, 0)),
                  pl.BlockSpec((1, D, tn), lambda l, j: (l, 0, j)),
                  pl.BlockSpec((1, 1, tn), lambda l, j: (l, 0, j))],
        out_specs=pl.BlockSpec((1, MOD_ROWS, tn), lambda l, j: (l, 0, j)),
        out_shape=jax.ShapeDtypeStruct((depth, MOD_ROWS, n6), F32),
        compiler_params=_params(2),
        name="ada_mod",
    )(xin, w_ada, b_ada.reshape(depth, 1, n6))


def _mod_spec(dm, tile):
    n6 = 6 * dm.D
    return pl.BlockSpec((1, 1, n6), lambda i, *_: (jnp.minimum(i * tile // dm.S, dm.B), 0, 0))


def _modulate_kernel(h_ref, mod_ref, o_ref, *, D):
    mod = mod_ref[0]
    o_ref[...] = (h_ref[...] * (1.0 + mod[:, D:2 * D]) + mod[:, 0:D]).astype(BF16)


def _modulate(dm, h, mod):
    tr = dm.tr
    return pl.pallas_call(
        functools.partial(_modulate_kernel, D=dm.D),
        grid=(dm.rows // tr,),
        in_specs=[pl.BlockSpec((tr, dm.D), lambda i: (i, 0)), _mod_spec(dm, tr)],
        out_specs=pl.BlockSpec((tr, dm.D), lambda i: (i, 0)),
        out_shape=jax.ShapeDtypeStruct((dm.rows, dm.D), BF16),
        compiler_params=_params(1),
        name="modulate_in",
    )(h, mod)


def _rope_tables(dm):
    t = np.arange(dm.S)
    pos = np.stack([t // GRID_W, t % GRID_W], axis=-1).astype(np.float32)
    inv_freq = jnp.asarray(ROPE_THETA, F32) ** (-jnp.arange(ROPE_FREQS, dtype=F32) / ROPE_FREQS)
    ang = jnp.asarray(pos)[:, :, None] * inv_freq
    cos, sin = jnp.cos(ang), jnp.sin(ang)
    cos_t = jnp.concatenate([cos, cos], axis=-1).reshape(dm.S, HEAD_DIM)
    sin_t = jnp.concatenate([-sin, sin], axis=-1).reshape(dm.S, HEAD_DIM)
    cos_t = jnp.concatenate([cos_t, jnp.ones((dm.tm, HEAD_DIM), F32)], axis=0)
    sin_t = jnp.concatenate([sin_t, jnp.zeros((dm.tm, HEAD_DIM), F32)], axis=0)
    return cos_t, sin_t


def _inproj_kernel(a_ref, w_ref, cos_ref, sin_ref, o_ref, *, tn, n_rope_q, n_rope, nat_q0, nat_q1):
    j = pl.program_id(1)
    acc = jnp.dot(a_ref[...], w_ref[...], preferred_element_type=F32)
    is_rope = j < n_rope
    is_natq = jnp.logical_and(j >= nat_q0, j < nat_q1)

    @pl.when(is_rope)
    def _():
        cos = cos_ref[...]
        sin = sin_ref[...]
        scale = jnp.where(j < n_rope_q, ATTN_SCALE, 1.0).astype(F32)
        lane = lax.broadcasted_iota(jnp.int32, cos.shape, 1)
        first_half = (lane % (2 * ROPE_FREQS)) < ROPE_FREQS
        for c in range(tn // HEAD_DIM):
            x = acc[:, c * HEAD_DIM:(c + 1) * HEAD_DIM]
            partner = jnp.where(first_half,
                                pltpu.roll(x, HEAD_DIM - ROPE_FREQS, 1),
                                pltpu.roll(x, ROPE_FREQS, 1))
            o_ref[:, c * HEAD_DIM:(c + 1) * HEAD_DIM] = ((x * cos + partner * sin) * scale).astype(BF16)

    @pl.when(is_natq)
    def _():
        o_ref[...] = (acc * ATTN_SCALE).astype(BF16)

    @pl.when(jnp.logical_not(jnp.logical_or(is_rope, is_natq)))
    def _():
        o_ref[...] = acc.astype(BF16)


def _inproj(dm, a, w_all, layer, cos_t, sin_t):
    tm, tn = dm.tm, dm.tn
    lat_tiles = dm.n_lat // tm
    per_seq = dm.S // tm

    def tab_map(i, j):
        return (jnp.where(i < lat_tiles, i % per_seq, per_seq), 0)

    kern = functools.partial(_inproj_kernel, tn=tn, n_rope_q=dm.swa_q // tn, n_rope=dm.off_swa_v // tn,
                             nat_q0=dm.off_nat_q // tn, nat_q1=dm.off_nat_k // tn)
    return pl.pallas_call(
        kern,
        grid=(dm.rows // tm, dm.in_cols // tn),
        in_specs=[pl.BlockSpec((tm, dm.D), lambda i, j: (i, 0)),
                  pl.BlockSpec((None, dm.D, tn), lambda i, j: (layer, 0, j)),
                  pl.BlockSpec((tm, HEAD_DIM), tab_map),
                  pl.BlockSpec((tm, HEAD_DIM), tab_map)],
        out_specs=pl.BlockSpec((tm, tn), lambda i, j: (i, j)),
        out_shape=jax.ShapeDtypeStruct((dm.rows, dm.in_cols), BF16),
        compiler_params=_params(2),
        name="in_proj",
    )(a, w_all, cos_t, sin_t)


def _mm_kernel(*refs, k_splits):
    a_refs, w_ref, o_ref = refs[:-2], refs[-2], refs[-1]
    acc = None
    k0 = 0
    for a_ref, kw in zip(a_refs, k_splits):
        part = jnp.dot(a_ref[...], w_ref[k0:k0 + kw, :], preferred_element_type=F32)
        acc = part if acc is None else acc + part
        k0 += kw
    o_ref[...] = acc.astype(o_ref.dtype)


def _matmul(a_list, w_all, layer, rows, tm, tn, out_dtype, name):
    _, K, N = w_all.shape
    k_splits = tuple(a.shape[1] for a in a_list)
    assert sum(k_splits) == K
    in_specs = [pl.BlockSpec((tm, kw), lambda i, j: (i, 0)) for kw in k_splits]
    in_specs.append(pl.BlockSpec((None, K, tn), lambda i, j: (layer, 0, j)))
    return pl.pallas_call(
        functools.partial(_mm_kernel, k_splits=k_splits),
        grid=(rows // tm, N // tn),
        in_specs=in_specs,
        out_specs=pl.BlockSpec((tm, tn), lambda i, j: (i, j)),
        out_shape=jax.ShapeDtypeStruct((rows, N), out_dtype),
        compiler_params=_params(2),
        name=name,
    )(*a_list, w_all)


def _dot_nt(a, b):
    return lax.dot_general(a, b, (((1,), (1,)), ((), ())), preferred_element_type=F32)


def _rms_gain(o, g):
    ms = jnp.mean(o * o, axis=-1, keepdims=True)
    return (o * lax.rsqrt(ms + RMS_EPS) * g).astype(BF16)


def _swa_mask_table():
    qi = np.arange(SWA_BLOCK)[:, None]
    kj = np.arange(SWA_BLOCK)[None, :]
    prev_ok = kj >= qi
    cur_ok = np.ones((SWA_BLOCK, SWA_BLOCK), bool)
    next_ok = kj <= qi
    none = np.zeros((SWA_BLOCK, SWA_BLOCK), bool)
    kinds = [np.concatenate([none, cur_ok, next_ok], 1),
             np.concatenate([prev_ok, cur_ok, next_ok], 1),
             np.concatenate([prev_ok, cur_ok, none], 1),
             np.concatenate([none, none, none], 1)]
    return jnp.asarray(np.where(np.stack(kinds), 0.0, NEG_INF).astype(np.float32))


def _swa_kernel(sink_ref, q_ref, kp_ref, kc_ref, kn_ref, vp_ref, vc_ref, vn_ref, kx_ref, vx_ref,
                mask_ref, g_ref, o_ref, acc_ref, *, kv_heads):
    n_ctx = kx_ref.shape[0]
    mask = jnp.concatenate([mask_ref[0], jnp.zeros((SWA_BLOCK, n_ctx), F32)], axis=1)
    nk = mask.shape[1]
    for hk in range(kv_heads):
        ks = slice(hk * HEAD_DIM, (hk + 1) * HEAD_DIM)
        q = jnp.concatenate([q_ref[:, (hk * SWA_GROUP + g) * HEAD_DIM:(hk * SWA_GROUP + g + 1) * HEAD_DIM]
                             for g in range(SWA_GROUP)], axis=0)
        kb = jnp.concatenate([kp_ref[:, ks], kc_ref[:, ks], kn_ref[:, ks], kx_ref[:, ks]], axis=0)
        vb = jnp.concatenate([vp_ref[:, ks], vc_ref[:, ks], vn_ref[:, ks], vx_ref[:, ks]], axis=0)
        vb = jnp.concatenate([vb, jnp.ones((nk, HEAD_DIM), BF16)], axis=1)
        s = _dot_nt(q, kb)
        s = (s.reshape(SWA_GROUP, SWA_BLOCK, nk) + mask[None]).reshape(SWA_GROUP * SWA_BLOCK, nk)
        sink = jnp.concatenate([jnp.full((SWA_BLOCK, 1), sink_ref[hk * SWA_GROUP + g], F32)
                                for g in range(SWA_GROUP)], axis=0)
        m = jnp.maximum(jnp.max(s, axis=-1, keepdims=True), sink)
        p = jnp.exp(s - m)
        o = jnp.dot(p.astype(BF16), vb, preferred_element_type=F32)
        denom = o[:, HEAD_DIM:HEAD_DIM + 1] + jnp.exp(sink - m)
        o = o[:, :HEAD_DIM] / denom
        for g in range(SWA_GROUP):
            h = hk * SWA_GROUP + g
            acc_ref[:, h * HEAD_DIM:(h + 1) * HEAD_DIM] = o[g * SWA_BLOCK:(g + 1) * SWA_BLOCK]
    o_ref[...] = _rms_gain(acc_ref[...], g_ref[...])


def _swa(dm, u, sink, mask_tab, gain):
    nb = dm.S // SWA_BLOCK
    lat_blocks = dm.B * nb
    ctx_per = dm.L // SWA_BLOCK
    n_blocks = lat_blocks + dm.B * ctx_per
    kvw = dm.swa_kv
    k_col = dm.off_swa_k // kvw
    v_col = dm.off_swa_v // kvw
    ctx_row0 = dm.n_lat // dm.L

    def batch_of(g):
        return jnp.where(g < lat_blocks, g // nb, (g - lat_blocks) // ctx_per)

    def nbr(g, d):
        n = g % nb
        return jnp.where(g < lat_blocks, (g // nb) * nb + jnp.clip(n + d, 0, nb - 1), g)

    def kind(g):
        n = g % nb
        return jnp.where(g < lat_blocks, jnp.where(n == 0, 0, jnp.where(n == nb - 1, 2, 1)), 3)

    blk = lambda col, d: pl.BlockSpec((SWA_BLOCK, kvw), lambda g: (nbr(g, d), col))
    ctx = lambda col: pl.BlockSpec((dm.L, kvw), lambda g: (ctx_row0 + batch_of(g), col))
    in_specs = [pl.BlockSpec(memory_space=pltpu.SMEM),
                pl.BlockSpec((SWA_BLOCK, dm.swa_q), lambda g: (g, 0)),
                blk(k_col, -1), blk(k_col, 0), blk(k_col, 1),
                blk(v_col, -1), blk(v_col, 0), blk(v_col, 1),
                ctx(k_col), ctx(v_col),
                pl.BlockSpec((1, SWA_BLOCK, 3 * SWA_BLOCK), lambda g: (kind(g), 0, 0)),
                pl.BlockSpec((1, dm.swa_q), lambda g: (0, 0))]
    return pl.pallas_call(
        functools.partial(_swa_kernel, kv_heads=dm.swa_kv_heads),
        grid=(n_blocks,),
        in_specs=in_specs,
        out_specs=pl.BlockSpec((SWA_BLOCK, dm.swa_q), lambda g: (g, 0)),
        out_shape=jax.ShapeDtypeStruct((dm.rows, dm.swa_q), BF16),
        scratch_shapes=[pltpu.VMEM((SWA_BLOCK, dm.swa_q), F32)],
        compiler_params=_params(1),
        name="window_attn",
    )(sink, u, u, u, u, u, u, u, u, u, mask_tab, gain)


def _nat_bias_tables(dm, rpb_all):
    nq, nk = NAT_QROWS, 3 * NAT_QROWS
    i = np.arange(nq)[:, None]
    j = np.arange(nk)[None, :]
    row_ok = [(j >= nq) & (j < nq + NAT_ROWS) & (i >= 0),
              (j - i >= 0) & (j - i < NAT_ROWS),
              (j >= 0) & (j < NAT_ROWS) & (i >= 0)]
    dr = np.clip(j - i + (NAT_ROWS - 1) - nq, 0, 2 * NAT_ROWS - 2)
    t_row = np.zeros((3, 2 * NAT_ROWS - 1, nq, nk), np.float32)
    for k in range(3):
        for a in range(nq):
            for b in range(nk):
                if row_ok[k][a, b]:
                    t_row[k, dr[a, b], a, b] = 1.0
    cq = np.arange(GRID_W)[:, None]
    kc = np.arange(GRID_W)[None, :]
    cs = np.clip(cq - NAT_COLS // 2, 0, GRID_W - NAT_COLS)
    col_ok = (kc >= cs) & (kc < cs + NAT_COLS)
    dc = np.clip(kc - cq + NAT_COLS - 1, 0, 2 * NAT_COLS - 2)
    t_col = np.zeros((2 * NAT_COLS - 1, GRID_W, GRID_W), np.float32)
    for a in range(GRID_W):
        for b in range(GRID_W):
            if col_ok[a, b]:
                t_col[dc[a, b], a, b] = 1.0
    ok = jnp.asarray(np.einsum('trij,cqk->tiqjk', t_row, t_col) > 0.5)
    t_row, t_col = jnp.asarray(t_row), jnp.asarray(t_col)
    val = jnp.einsum('lhrc,trij,cqk->lthiqjk', rpb_all, t_row, t_col, precision=lax.Precision.HIGHEST)
    bias = jnp.where(ok[None, :, None], val.astype(BF16), jnp.asarray(NEG_INF, BF16))
    depth, _, H = bias.shape[:3]
    bias = bias.reshape(depth, 3, H, NAT_QBLOCK, 3 * NAT_QBLOCK)
    none = jnp.full((depth, 1, H, NAT_QBLOCK, 3 * NAT_QBLOCK), NEG_INF, BF16)
    return jnp.concatenate([bias, none], axis=1)


def _nat_kernel(q_ref, kp_ref, kc_ref, kn_ref, vp_ref, vc_ref, vn_ref, kx_ref, vx_ref,
                bias_ref, g_ref, o_ref, acc_ref, *, heads):
    n_ctx = kx_ref.shape[0]
    no_bias = jnp.zeros((NAT_QBLOCK, n_ctx), F32)
    nk = 3 * NAT_QBLOCK + n_ctx
    for h in range(heads):
        hs = slice(h * HEAD_DIM, (h + 1) * HEAD_DIM)
        q = q_ref[:, hs]
        kb = jnp.concatenate([kp_ref[:, hs], kc_ref[:, hs], kn_ref[:, hs], kx_ref[:, hs]], axis=0)
        vb = jnp.concatenate([vp_ref[:, hs], vc_ref[:, hs], vn_ref[:, hs], vx_ref[:, hs]], axis=0)
        vb = jnp.concatenate([vb, jnp.ones((nk, HEAD_DIM), BF16)], axis=1)
        s = _dot_nt(q, kb) + jnp.concatenate([bias_ref[0, h].astype(F32), no_bias], axis=1)
        m = jnp.max(s, axis=-1, keepdims=True)
        p = jnp.exp(s - m)
        o = jnp.dot(p.astype(BF16), vb, preferred_element_type=F32)
        acc_ref[:, hs] = o[:, :HEAD_DIM] / o[:, HEAD_DIM:HEAD_DIM + 1]
    o_ref[...] = _rms_gain(acc_ref[...], g_ref[...])


def _nat(dm, u, bias_tab, gain):
    nb = dm.S // NAT_QBLOCK
    lat_blocks = dm.B * nb
    ctx_per = dm.L // NAT_QBLOCK
    n_blocks = lat_blocks + dm.B * ctx_per
    w = dm.nat_w
    q_col, k_col, v_col = dm.off_nat_q // w, dm.off_nat_k // w, dm.off_nat_v // w
    ctx_row0 = dm.n_lat // dm.L

    def batch_of(g):
        return jnp.where(g < lat_blocks, g // nb, (g - lat_blocks) // ctx_per)

    def nbr(g, d):
        n = g % nb
        return jnp.where(g < lat_blocks, (g // nb) * nb + jnp.clip(n + d, 0, nb - 1), g)

    def kind(g):
        n = g % nb
        return jnp.where(g < lat_blocks, jnp.where(n == 0, 0, jnp.where(n == nb - 1, 2, 1)), 3)

    blk = lambda col, d: pl.BlockSpec((NAT_QBLOCK, w), lambda g: (nbr(g, d), col))
    ctx = lambda col: pl.BlockSpec((dm.L, w), lambda g: (ctx_row0 + batch_of(g), col))
    in_specs = [pl.BlockSpec((NAT_QBLOCK, w), lambda g: (g, q_col)),
                blk(k_col, -1), blk(k_col, 0), blk(k_col, 1),
                blk(v_col, -1), blk(v_col, 0), blk(v_col, 1),
                ctx(k_col), ctx(v_col),
                pl.BlockSpec((1, dm.nat_heads, NAT_QBLOCK, 3 * NAT_QBLOCK), lambda g: (kind(g), 0, 0, 0)),
                pl.BlockSpec((1, w), lambda g: (0, 0))]
    return pl.pallas_call(
        functools.partial(_nat_kernel, heads=dm.nat_heads),
        grid=(n_blocks,),
        in_specs=in_specs,
        out_specs=pl.BlockSpec((NAT_QBLOCK, w), lambda g: (g, 0)),
        out_shape=jax.ShapeDtypeStruct((dm.rows, w), BF16),
        scratch_shapes=[pltpu.VMEM((NAT_QBLOCK, w), F32)],
        compiler_params=_params(1),
        name="nbr_attn",
    )(u, u, u, u, u, u, u, u, u, bias_tab, gain)


CONV_TILE = 256
HALO = 16


def _conv_kernel(u_ref, prev_ref, next_ref, w_ref, b_ref, g_ref, o_ref, *, cw, tiles_per_seq, lat_tiles):
    i = pl.program_id(0)
    is_ctx = i >= lat_tiles
    n = i % tiles_per_seq
    has_prev = jnp.logical_and(jnp.logical_not(is_ctx), n != 0)
    has_next = jnp.logical_and(jnp.logical_not(is_ctx), n != tiles_per_seq - 1)
    x = u_ref[:, 0:cw].astype(F32)
    bg = u_ref[:, cw:2 * cw].astype(F32)
    cg = u_ref[:, 2 * cw:3 * cw].astype(F32)
    z = cg * x
    zp = (prev_ref[HALO - 1:HALO, 2 * cw:3 * cw].astype(F32) * prev_ref[HALO - 1:HALO, 0:cw].astype(F32))
    zn = (next_ref[0:1, 2 * cw:3 * cw].astype(F32) * next_ref[0:1, 0:cw].astype(F32))
    zp = jnp.where(has_prev, zp, 0.0)
    zn = jnp.where(has_next, zn, 0.0)
    row = lax.broadcasted_iota(jnp.int32, z.shape, 0)
    z_m1 = jnp.where(row == 0, zp, pltpu.roll(z, 1, 0))
    z_p1 = jnp.where(row == CONV_TILE - 1, zn, pltpu.roll(z, CONV_TILE - 1, 0))
    w = w_ref[...]
    conv = b_ref[...] + z_m1 * w[0:1] + z * w[1:2] + z_p1 * w[2:3]
    o_ref[...] = _rms_gain(bg * conv, g_ref[...])


def _conv(dm, u, w, b, gain):
    assert dm.L == CONV_TILE, "context sequences are one convolution tile"
    cw = dm.conv_w
    tiles = dm.rows // CONV_TILE
    lat_tiles = dm.n_lat // CONV_TILE
    col = dm.off_conv // (3 * cw)
    per_halo = CONV_TILE // HALO
    last_halo = dm.rows // HALO - 1
    kern = functools.partial(_conv_kernel, cw=cw, tiles_per_seq=dm.S // CONV_TILE, lat_tiles=lat_tiles)
    return pl.pallas_call(
        kern,
        grid=(tiles,),
        in_specs=[pl.BlockSpec((CONV_TILE, 3 * cw), lambda i: (i, col)),
                  pl.BlockSpec((HALO, 3 * cw), lambda i: (jnp.maximum(i * per_halo - 1, 0), col)),
                  pl.BlockSpec((HALO, 3 * cw), lambda i: (jnp.minimum((i + 1) * per_halo, last_halo), col)),
                  pl.BlockSpec((3, cw), lambda i: (0, 0)),
                  pl.BlockSpec((1, cw), lambda i: (0, 0)),
                  pl.BlockSpec((1, cw), lambda i: (0, 0))],
        out_specs=pl.BlockSpec((CONV_TILE, cw), lambda i: (i, 0)),
        out_shape=jax.ShapeDtypeStruct((dm.rows, cw), BF16),
        compiler_params=_params(1),
        name="gated_conv",
    )(u, u, u, w, b, gain)


def _layer_norm(y, g, b):
    mu = jnp.mean(y, axis=-1, keepdims=True)
    yc = y - mu
    var = jnp.mean(yc * yc, axis=-1, keepdims=True)
    return yc * lax.rsqrt(var + LN_EPS) * g + b


def _route(logits):
    lane = lax.broadcasted_iota(jnp.int32, logits.shape, 1).astype(F32)
    big = float(ROUTER_LANES)
    is_g = lane < N_GROUPS
    gl = jnp.where(is_g, logits, NEG_INF)
    gmax = jnp.max(gl, axis=-1, keepdims=True)
    ge = jnp.where(is_g, jnp.exp(gl - gmax), 0.0)
    gprob = ge / jnp.sum(ge, axis=-1, keepdims=True)
    g_p = jnp.max(gprob, axis=-1, keepdims=True)
    g_idx = jnp.min(jnp.where(jnp.logical_and(is_g, gprob == g_p), lane, big), axis=-1, keepdims=True)
    lo = N_GROUPS + g_idx * EXPERTS_PER_GROUP
    in_grp = jnp.logical_and(lane >= lo, lane < lo + EXPERTS_PER_GROUP)
    el = jnp.where(in_grp, logits, NEG_INF)
    v1 = jnp.max(el, axis=-1, keepdims=True)
    i1 = jnp.min(jnp.where(jnp.logical_and(in_grp, el == v1), lane, big), axis=-1, keepdims=True)
    el2 = jnp.where(lane == i1, NEG_INF, el)
    v2 = jnp.max(el2, axis=-1, keepdims=True)
    rest = jnp.logical_and(in_grp, lane != i1)
    i2 = jnp.min(jnp.where(jnp.logical_and(rest, el2 == v2), lane, big), axis=-1, keepdims=True)
    e2 = jnp.exp(v2 - v1)
    den = 1.0 + e2
    w1 = g_p / den
    w2 = g_p * (e2 / den)
    gates = jnp.where(lane == i1, w1, jnp.where(lane == i2, w2, 0.0))
    return jnp.where(lane == ROUTE_GROUP_LANE, g_idx, gates)


HIGH_HALF = 0xFFFF0000


def _pack_bf16_pairs(x):
    w = x.shape[1] // 2
    hi = lax.bitcast_convert_type(x[:, :w].astype(BF16).astype(F32), jnp.uint32)
    lo = lax.bitcast_convert_type(x[:, w:].astype(BF16).astype(F32), jnp.uint32)
    return hi | (lo >> 16)


def _unpack_bf16_pairs(p):
    hi = lax.bitcast_convert_type(p & jnp.uint32(HIGH_HALF), F32)
    lo = lax.bitcast_convert_type(p << 16, F32)
    return hi, lo


def _start_row_gather(idx_ref, idx0, src_hbm, dst_ref, sem, row0, n_rows):
    def issue(k, carry):
        j = row0 + k
        row = idx_ref[idx0 + j]
        pltpu.make_async_copy(src_hbm.at[pl.ds(row, 1)], dst_ref.at[pl.ds(j, 1)], sem).start()
        return carry

    lax.fori_loop(0, n_rows, issue, 0, unroll=8)


def _wait_row_gather(src_hbm, dst_ref, sem):
    pltpu.make_async_copy(src_hbm.at[pl.ds(0, dst_ref.shape[0])], dst_ref, sem).wait()


def _ln1_kernel(h_ref, mix_ref, mod_ref, g_ref, b_ref, wr_ref, br_ref, h_out, t_out, route_out, *, D, alpha):
    mod = mod_ref[0]
    y = alpha * h_ref[...] + mod[:, 2 * D:3 * D] * mix_ref[...].astype(F32)
    hn = _layer_norm(y, g_ref[...], b_ref[...])
    h_out[...] = hn
    t = hn * (1.0 + mod[:, 4 * D:5 * D]) + mod[:, 3 * D:4 * D]
    logits = jnp.dot(t, wr_ref[...], preferred_element_type=F32, precision=lax.Precision.HIGHEST) + br_ref[...]
    route = _route(logits)
    route_out[...] = route
    t_out[:, 0:D // 2] = _pack_bf16_pairs(t)
    t_out[:, D // 2:] = lax.bitcast_convert_type(route, jnp.uint32)


def _ln1(dm, rows, h, mix, mod, g, b, wr, br):
    tr, D = dm.tr, dm.D
    row = lambda width: pl.BlockSpec((tr, width), lambda i: (i, 0))
    vec = lambda width: pl.BlockSpec((1, width), lambda i: (0, 0))
    return pl.pallas_call(
        functools.partial(_ln1_kernel, D=D, alpha=dm.alpha),
        grid=(rows // tr,),
        in_specs=[row(D), row(D), _mod_spec(dm, tr), vec(D), vec(D),
                  pl.BlockSpec((D, ROUTER_LANES), lambda i: (0, 0)), vec(ROUTER_LANES)],
        out_specs=[row(D), row(dm.packed_w), row(ROUTER_LANES)],
        out_shape=[jax.ShapeDtypeStruct((rows, D), F32), jax.ShapeDtypeStruct((rows, dm.packed_w), jnp.uint32),
                   jax.ShapeDtypeStruct((rows, ROUTER_LANES), F32)],
        compiler_params=_params(1),
        name="mix_residual_norm_route",
    )(h, mix, mod, g, b, wr, br)


def _ln2_kernel(pos_ref, h_ref, f_hbm, mod_ref, modn_ref, g_ref, b_ref, h_out, a_out, fbuf, sem, *,
                D, alpha, tr, n_tiles):
    i = pl.program_id(0)
    slot = i % 2

    @pl.when(i == 0)
    def _():
        _start_row_gather(pos_ref, 0, f_hbm, fbuf.at[0], sem.at[0], 0, tr)

    @pl.when(i + 1 < n_tiles)
    def _():
        _start_row_gather(pos_ref, (i + 1) * tr, f_hbm, fbuf.at[1 - slot], sem.at[1 - slot], 0, tr)

    _wait_row_gather(f_hbm, fbuf.at[slot], sem.at[slot])
    mod = mod_ref[0]
    f_hi, f_lo = _unpack_bf16_pairs(fbuf[slot])
    ffn = jnp.concatenate([f_hi, f_lo], axis=1)
    y = alpha * h_ref[...] + mod[:, 5 * D:6 * D] * ffn
    hn = _layer_norm(y, g_ref[...], b_ref[...])
    h_out[...] = hn
    if a_out is not None:
        modn = modn_ref[0]
        a_out[...] = (hn * (1.0 + modn[:, D:2 * D]) + modn[:, 0:D]).astype(BF16)


def _ln2_last_kernel(pos_ref, h_ref, f_hbm, mod_ref, g_ref, b_ref, h_out, fbuf, sem, **kw):
    _ln2_kernel(pos_ref, h_ref, f_hbm, mod_ref, None, g_ref, b_ref, h_out, None, fbuf, sem, **kw)


def _ln2(dm, rows, pos, h, ffn_sorted, mod, mod_next, g, b):
    tr, D = dm.tr, dm.D
    n_tiles = rows // tr
    row = pl.BlockSpec((tr, D), lambda i, p: (i, 0))
    vec = pl.BlockSpec((1, D), lambda i, p: (0, 0))
    hbm = pl.BlockSpec(memory_space=pl.ANY)
    scratch = [pltpu.VMEM((2, tr, D // 2), jnp.uint32), pltpu.SemaphoreType.DMA((2,))]
    kw = dict(D=D, alpha=dm.alpha, tr=tr, n_tiles=n_tiles)
    if mod_next is None:
        return pl.pallas_call(
            functools.partial(_ln2_last_kernel, **kw),
            grid_spec=pltpu.PrefetchScalarGridSpec(
                num_scalar_prefetch=1, grid=(n_tiles,),
                in_specs=[row, hbm, _mod_spec(dm, tr), vec, vec],
                out_specs=row, scratch_shapes=scratch),
            out_shape=jax.ShapeDtypeStruct((rows, D), F32),
            compiler_params=_params(1),
            name="ffn_residual_norm_last",
        )(pos, h, ffn_sorted, mod, g, b), None
    return pl.pallas_call(
        functools.partial(_ln2_kernel, **kw),
        grid_spec=pltpu.PrefetchScalarGridSpec(
            num_scalar_prefetch=1, grid=(n_tiles,),
            in_specs=[row, hbm, _mod_spec(dm, tr), _mod_spec(dm, tr), vec, vec],
            out_specs=[row, row], scratch_shapes=scratch),
        out_shape=[jax.ShapeDtypeStruct((rows, D), F32), jax.ShapeDtypeStruct((rows, D), BF16)],
        compiler_params=_params(1),
        name="ffn_residual_norm",
    )(pos, h, ffn_sorted, mod, mod_next, g, b)


def _dispatch_plan(rows, route):
    tm = EXPERT_TILE
    n_tiles = rows // tm + N_GROUPS
    i32 = jnp.int32
    gidx = route[:, ROUTE_GROUP_LANE].astype(i32)
    onehot = (gidx[:, None] == jnp.arange(N_GROUPS, dtype=i32)[None]).astype(i32)
    csum = jnp.cumsum(onehot, axis=0)
    counts = csum[-1]
    rank = jnp.sum(onehot * csum, axis=1) - 1
    padded = (counts + tm - 1) // tm * tm
    pstart = jnp.cumsum(padded) - padded
    ustart = jnp.cumsum(counts) - counts
    total = jnp.sum(padded)
    pos = jnp.sum(onehot * pstart[None], axis=1) + rank
    order = jnp.argsort(gidx, stable=True).astype(i32)
    slot = jnp.arange(n_tiles * tm, dtype=i32)
    sgrp = jnp.sum((slot[:, None] >= pstart[None, 1:]).astype(i32), axis=1)
    shot = (sgrp[:, None] == jnp.arange(N_GROUPS, dtype=i32)[None]).astype(i32)
    k = slot - jnp.sum(shot * pstart[None], axis=1)
    valid = jnp.logical_and(k < jnp.sum(shot * counts[None], axis=1), slot < total)
    sorted_at = jnp.clip(jnp.sum(shot * ustart[None], axis=1) + k, 0, rows - 1)
    src = jnp.where(valid, order[sorted_at], slot % rows)
    tile_slot = jnp.arange(n_tiles, dtype=i32) * tm
    tgrp = jnp.where(tile_slot < total, sgrp[::tm], N_GROUPS)
    return src, pos.astype(i32), tgrp.astype(i32)


def _experts_kernel(src_ref, tgrp_ref, t_hbm, wg_ref, wu_ref, wd_ref, o_ref, tbuf, sem, tb_ref, gate_ref, acc_ref,
                    *, D, tm, n_tiles):
    i = pl.program_id(0)
    e = pl.program_id(1)
    slot = i % 2
    grp = tgrp_ref[i]
    active = grp < N_GROUPS
    half = D // 2
    part_rows = tm // EXPERTS_PER_GROUP

    @pl.when(jnp.logical_and(i == 0, e == 0))
    def _():
        _start_row_gather(src_ref, 0, t_hbm, tbuf.at[0], sem.at[0], 0, tm)

    @pl.when(e == 0)
    def _():
        _wait_row_gather(t_hbm, tbuf.at[slot], sem.at[slot])
        hi, lo = _unpack_bf16_pairs(tbuf[slot, :, 0:half])
        tb_ref[:, 0:half] = hi.astype(BF16)
        tb_ref[:, half:] = lo.astype(BF16)
        gate_ref[...] = lax.bitcast_convert_type(tbuf[slot, :, half:], F32)

    @pl.when(i + 1 < n_tiles)
    def _():
        _start_row_gather(src_ref, (i + 1) * tm, t_hbm, tbuf.at[1 - slot], sem.at[1 - slot],
                          e * part_rows, part_rows)

    @pl.when(active)
    def _():
        t = tb_ref[...]
        a = jnp.dot(t, wg_ref[...], preferred_element_type=F32)
        b = jnp.dot(t, wu_ref[...], preferred_element_type=F32)
        gates = gate_ref[...]
        lane = lax.broadcasted_iota(jnp.int32, gates.shape, 1)
        gcol = jnp.sum(jnp.where(lane == N_GROUPS + grp * EXPERTS_PER_GROUP + e, gates, 0.0),
                       axis=-1, keepdims=True)
        hid = (a * jax.nn.sigmoid(a) * b * gcol).astype(BF16)
        part = jnp.dot(hid, wd_ref[...], preferred_element_type=F32)

        @pl.when(e == 0)
        def _():
            acc_ref[...] = part

        @pl.when(e > 0)
        def _():
            acc_ref[...] += part

    last = e == EXPERTS_PER_GROUP - 1

    @pl.when(jnp.logical_and(last, active))
    def _():
        o_ref[...] = _pack_bf16_pairs(acc_ref[...])

    @pl.when(jnp.logical_and(last, jnp.logical_not(active)))
    def _():
        o_ref[...] = jnp.zeros(o_ref.shape, o_ref.dtype)


def _experts(dm, layer, src, tgrp, t_packed, wg_all, wu_all, wd_all):
    tm, D, F = EXPERT_TILE, dm.D, dm.d_expert
    n_tiles = tgrp.shape[0]

    def w_map(i, e, sr, tg):
        return (layer, jnp.minimum(tg[i], N_GROUPS - 1) * EXPERTS_PER_GROUP + e, 0, 0)

    return pl.pallas_call(
        functools.partial(_experts_kernel, D=D, tm=tm, n_tiles=n_tiles),
        grid_spec=pltpu.PrefetchScalarGridSpec(
            num_scalar_prefetch=2,
            grid=(n_tiles, EXPERTS_PER_GROUP),
            in_specs=[pl.BlockSpec(memory_space=pl.ANY),
                      pl.BlockSpec((None, None, D, F), w_map),
                      pl.BlockSpec((None, None, D, F), w_map),
                      pl.BlockSpec((None, None, F, D), w_map)],
            out_specs=pl.BlockSpec((tm, D // 2), lambda i, e, sr, tg: (i, 0)),
            scratch_shapes=[pltpu.VMEM((2, tm, dm.packed_w), jnp.uint32), pltpu.SemaphoreType.DMA((2,)),
                            pltpu.VMEM((tm, D), BF16), pltpu.VMEM((tm, ROUTER_LANES), F32),
                            pltpu.VMEM((tm, D), F32)]),
        out_shape=jax.ShapeDtypeStruct((n_tiles * tm, D // 2), jnp.uint32),
        compiler_params=_params(2),
        name="experts",
    )(src, tgrp, t_packed, wg_all, wu_all, wd_all)


def kernel(x, c, ctx, c_ctx, w_ada, b_ada, w_in, conv_w, conv_b, attn_sink, nat_rpb, mix_norm_g, w_out,
           ln1_g, ln1_b, w_router_group, b_router_group, w_router_expert, b_router_expert,
           w_gate, w_up, w_down, ln2_g, ln2_b):
    B, S, D = x.shape
    L = ctx.shape[1]
    depth = w_in.shape[0]
    dm = Dims(B, S, L, D, depth, w_gate.shape[-1])

    xin = jnp.concatenate([c, c_ctx[None], jnp.zeros((MOD_ROWS - B - 1, D), F32)], axis=0)
    mods = _ada_all(xin, w_ada, b_ada).reshape(depth, MOD_ROWS, 1, 6 * D)

    cos_t, sin_t = _rope_tables(dm)
    swa_mask = _swa_mask_table()
    nat_bias = _nat_bias_tables(dm, nat_rpb)

    w_in_b = w_in.astype(BF16)
    w_out_b = w_out.astype(BF16)
    w_gate_b = w_gate.astype(BF16)
    w_up_b = w_up.astype(BF16)
    w_down_b = w_down.astype(BF16)
    pad = jnp.zeros((depth, D, ROUTER_LANES - N_GROUPS - N_EXPERTS), F32)
    w_route = jnp.concatenate([w_router_group, w_router_expert, pad], axis=-1)
    b_route = jnp.concatenate([b_router_group, b_router_expert, pad[:, 0]], axis=-1).reshape(depth, 1, ROUTER_LANES)

    h = jnp.concatenate([x.reshape(B * S, D), ctx.reshape(B * L, D)], axis=0)
    a = _modulate(dm, h, mods[0])
    for i in range(depth):
        last = i == depth - 1
        rows = dm.n_lat if last else dm.rows
        u = _inproj(dm, a, w_in_b, i, cos_t, sin_t)
        gain = mix_norm_g[i].reshape(1, D)
        y_swa = _swa(dm, u, attn_sink[i], swa_mask, gain[:, :dm.swa_q])
        y_conv = _conv(dm, u, conv_w[i], conv_b[i].reshape(1, -1), gain[:, dm.swa_q:dm.swa_q + dm.conv_w])
        y_nat = _nat(dm, u, nat_bias[i], gain[:, dm.swa_q + dm.conv_w:])
        mix = _matmul([y_swa, y_conv, y_nat], w_out_b, i, rows, dm.tm, 512, BF16, "out_proj")
        h_mid, t_packed, route = _ln1(dm, rows, h, mix, mods[i], ln1_g[i].reshape(1, D), ln1_b[i].reshape(1, D),
                                      w_route[i], b_route[i])
        src, pos, tgrp = _dispatch_plan(rows, route)
        ffn_sorted = _experts(dm, i, src, tgrp, t_packed, w_gate_b, w_up_b, w_down_b)
        h, a = _ln2(dm, rows, pos, h_mid, ffn_sorted, mods[i], None if last else mods[i + 1],
                    ln2_g[i].reshape(1, D), ln2_b[i].reshape(1, D))
    return h.reshape(B, S, D)
```

```python
import functools

import numpy as np
import jax
import jax.numpy as jnp
from jax import lax
from jax.experimental import pallas as pl
from jax.experimental.pallas import tpu as pltpu

F32 = jnp.float32
BF16 = jnp.bfloat16

HEAD_DIM = 128
LANES = 128
GRID_W = 64
ROPE_THETA = 10000.0
ROPE_FREQS = HEAD_DIM // 4
SWA_GROUP = 4
SWA_BLOCK = 128
NAT_ROWS = 8
NAT_COLS = 16
NAT_QROWS = 4
NAT_QBLOCK = NAT_QROWS * GRID_W
N_GROUPS = 4
EXPERTS_PER_GROUP = 4
N_EXPERTS = N_GROUPS * EXPERTS_PER_GROUP
ROUTER_LANES = 128
ROUTE_GROUP_LANE = 0
EXPERT_TILE = 512
LN_EPS = 1e-5
RMS_EPS = 1e-6
NEG_INF = -1e30
ATTN_SCALE = HEAD_DIM ** -0.5
MOD_ROWS = 8
VMEM_LIMIT = 56 * 1024 * 1024


def _params(n_axes):
    return pltpu.CompilerParams(dimension_semantics=("arbitrary",) * n_axes,
                                vmem_limit_bytes=VMEM_LIMIT)


class Dims:
    def __init__(self, B, S, L, D, depth, d_expert):
        self.B, self.S, self.L, self.D, self.depth, self.d_expert = B, S, L, D, depth, d_expert
        self.swa_q = D // 2
        self.swa_heads = self.swa_q // HEAD_DIM
        self.swa_kv_heads = self.swa_heads // SWA_GROUP
        self.swa_kv = self.swa_kv_heads * HEAD_DIM
        self.conv_w = D // 4
        self.nat_w = D // 4
        self.nat_heads = self.nat_w // HEAD_DIM
        self.off_swa_k = self.swa_q
        self.off_swa_v = self.off_swa_k + self.swa_kv
        self.off_conv = self.off_swa_v + self.swa_kv
        self.off_nat_q = self.off_conv + 3 * self.conv_w
        self.off_nat_k = self.off_nat_q + self.nat_w
        self.off_nat_v = self.off_nat_k + self.nat_w
        self.in_cols = self.off_nat_v + self.nat_w
        self.n_lat = B * S
        self.n_ctx = B * L
        self.rows = self.n_lat + self.n_ctx
        self.grid_rows = S // GRID_W
        self.packed_w = D // 2 + ROUTER_LANES
        assert self.rows % EXPERT_TILE == 0 and self.n_lat % EXPERT_TILE == 0
        self.alpha = (2.0 * depth) ** 0.25
        self.tm = 1024 if (S % 1024 == 0 and self.n_ctx % 1024 == 0) else 256
        self.tr = 256
        self.tn = self.swa_kv
        assert S % self.tm == 0 and self.n_ctx % self.tm == 0
        assert S % NAT_QBLOCK == 0 and L % NAT_QBLOCK == 0 and self.grid_rows >= 3 * NAT_QROWS
        assert S // SWA_BLOCK >= 3 and L % SWA_BLOCK == 0
        assert self.off_conv % (3 * self.conv_w) == 0
        assert B + 1 <= MOD_ROWS


def _ada_kernel(x_ref, w_ref, b_ref, o_ref):
    x = x_ref[...]
    act = x * jax.nn.sigmoid(x)
    a_hi = act.astype(BF16)
    a_lo = (act - a_hi.astype(F32)).astype(BF16)
    w = w_ref[0]
    w_hi = w.astype(BF16)
    w_lo = (w - w_hi.astype(F32)).astype(BF16)
    acc = jnp.dot(a_hi, w_hi, preferred_element_type=F32)
    acc += jnp.dot(a_lo, w_hi, preferred_element_type=F32)
    acc += jnp.dot(a_hi, w_lo, preferred_element_type=F32)
    o_ref[0] = acc + b_ref[0]


def _ada_all(xin, w_ada, b_ada):
    depth, D, n6 = w_ada.shape
    tn = 512
    return pl.pallas_call(
        _ada_kernel,
        grid=(depth, n6 // tn),
        in_specs=[pl.BlockSpec((MOD_ROWS, D), lambda l, j: (0, 0)),
                  pl.BlockSpec((1, D, tn), lambda l, j: (l, 0, j)),
                  pl.BlockSpec((1, 1, tn), lambda l, j: (l, 0, j))],
        out_specs=pl.BlockSpec((1, MOD_ROWS, tn), lambda l, j: (l, 0, j)),
        out_shape=jax.ShapeDtypeStruct((depth, MOD_ROWS, n6), F32),
        compiler_params=_params(2),
        name="ada_mod",
    )(xin, w_ada, b_ada.reshape(depth, 1, n6))


def _mod_spec(dm, tile):
    n6 = 6 * dm.D
    return pl.BlockSpec((1, 1, n6), lambda i, *_: (jnp.minimum(i * tile // dm.S, dm.B), 0, 0))


def _modulate_kernel(h_ref, mod_ref, o_ref, *, D):
    mod = mod_ref[0]
    o_ref[...] = (h_ref[...] * (1.0 + mod[:, D:2 * D]) + mod[:, 0:D]).astype(BF16)


def _modulate(dm, h, mod):
    tr = dm.tr
    return pl.pallas_call(
        functools.partial(_modulate_kernel, D=dm.D),
        grid=(dm.rows // tr,),
        in_specs=[pl.BlockSpec((tr, dm.D), lambda i: (i, 0)), _mod_spec(dm, tr)],
        out_specs=pl.BlockSpec((tr, dm.D), lambda i: (i, 0)),
        out_shape=jax.ShapeDtypeStruct((dm.rows, dm.D), BF16),
        compiler_params=_params(1),
        name="modulate_in",
    )(h, mod)


def _rope_tables(dm):
    t = np.arange(dm.S)
    pos = np.stack([t // GRID_W, t % GRID_W], axis=-1).astype(np.float32)
    inv_freq = jnp.asarray(ROPE_THETA, F32) ** (-jnp.arange(ROPE_FREQS, dtype=F32) / ROPE_FREQS)
    ang = jnp.asarray(pos)[:, :, None] * inv_freq
    cos, sin = jnp.cos(ang), jnp.sin(ang)
    cos_t = jnp.concatenate([cos, cos], axis=-1).reshape(dm.S, HEAD_DIM)
    sin_t = jnp.concatenate([-sin, sin], axis=-1).reshape(dm.S, HEAD_DIM)
    cos_t = jnp.concatenate([cos_t, jnp.ones((dm.tm, HEAD_DIM), F32)], axis=0)
    sin_t = jnp.concatenate([sin_t, jnp.zeros((dm.tm, HEAD_DIM), F32)], axis=0)
    return cos_t, sin_t


def _inproj_kernel(a_ref, w_ref, cos_ref, sin_ref, o_ref, wb_ref, *, tn, n_rope_q, n_rope, nat_q0, nat_q1):
    j = pl.program_id(0)

    @pl.when(pl.program_id(1) == 0)
    def _():
        wb_ref[...] = w_ref[...].astype(BF16)

    acc = jnp.dot(a_ref[...], wb_ref[...], preferred_element_type=F32)
    is_rope = j < n_rope
    is_natq = jnp.logical_and(j >= nat_q0, j < nat_q1)

    @pl.when(is_rope)
    def _():
        cos = cos_ref[...]
        sin = sin_ref[...]
        scale = jnp.where(j < n_rope_q, ATTN_SCALE, 1.0).astype(F32)
        lane = lax.broadcasted_iota(jnp.int32, cos.shape, 1)
        first_half = (lane % (2 * ROPE_FREQS)) < ROPE_FREQS
        for c in range(tn // HEAD_DIM):
            x = acc[:, c * HEAD_DIM:(c + 1) * HEAD_DIM]
            partner = jnp.where(first_half,
                                pltpu.roll(x, HEAD_DIM - ROPE_FREQS, 1),
                                pltpu.roll(x, ROPE_FREQS, 1))
            o_ref[:, c * HEAD_DIM:(c + 1) * HEAD_DIM] = ((x * cos + partner * sin) * scale).astype(BF16)

    @pl.when(is_natq)
    def _():
        o_ref[...] = (acc * ATTN_SCALE).astype(BF16)

    @pl.when(jnp.logical_not(jnp.logical_or(is_rope, is_natq)))
    def _():
        o_ref[...] = acc.astype(BF16)


def _inproj(dm, a, w_all, layer, cos_t, sin_t):
    tm, tn = dm.tm, dm.tn
    lat_tiles = dm.n_lat // tm
    per_seq = dm.S // tm

    def tab_map(j, i):
        return (jnp.where(i < lat_tiles, i % per_seq, per_seq), 0)

    kern = functools.partial(_inproj_kernel, tn=tn, n_rope_q=dm.swa_q // tn, n_rope=dm.off_swa_v // tn,
                             nat_q0=dm.off_nat_q // tn, nat_q1=dm.off_nat_k // tn)
    return pl.pallas_call(
        kern,
        grid=(dm.in_cols // tn, dm.rows // tm),
        in_specs=[pl.BlockSpec((tm, dm.D), lambda j, i: (i, 0)),
                  pl.BlockSpec((None, dm.D, tn), lambda j, i: (layer, 0, j)),
                  pl.BlockSpec((tm, HEAD_DIM), tab_map),
                  pl.BlockSpec((tm, HEAD_DIM), tab_map)],
        out_specs=pl.BlockSpec((tm, tn), lambda j, i: (i, j)),
        out_shape=jax.ShapeDtypeStruct((dm.rows, dm.in_cols), BF16),
        scratch_shapes=[pltpu.VMEM((dm.D, tn), BF16)],
        compiler_params=_params(2),
        name="in_proj",
    )(a, w_all, cos_t, sin_t)


def _mm_kernel(*refs, k_splits):
    a_refs, w_ref, o_ref, wb_ref = refs[:-3], refs[-3], refs[-2], refs[-1]

    @pl.when(pl.program_id(1) == 0)
    def _():
        wb_ref[...] = w_ref[...].astype(BF16)

    acc = None
    k0 = 0
    for a_ref, kw in zip(a_refs, k_splits):
        part = jnp.dot(a_ref[...], wb_ref[k0:k0 + kw, :], preferred_element_type=F32)
        acc = part if acc is None else acc + part
        k0 += kw
    o_ref[...] = acc.astype(o_ref.dtype)


def _matmul(a_list, w_all, layer, rows, tm, tn, out_dtype, name):
    _, K, N = w_all.shape
    k_splits = tuple(a.shape[1] for a in a_list)
    assert sum(k_splits) == K
    in_specs = [pl.BlockSpec((tm, kw), lambda j, i: (i, 0)) for kw in k_splits]
    in_specs.append(pl.BlockSpec((None, K, tn), lambda j, i: (layer, 0, j)))
    return pl.pallas_call(
        functools.partial(_mm_kernel, k_splits=k_splits),
        grid=(N // tn, rows // tm),
        in_specs=in_specs,
        out_specs=pl.BlockSpec((tm, tn), lambda j, i: (i, j)),
        out_shape=jax.ShapeDtypeStruct((rows, N), out_dtype),
        scratch_shapes=[pltpu.VMEM((K, tn), BF16)],
        compiler_params=_params(2),
        name=name,
    )(*a_list, w_all)


def _dot_nt(a, b):
    return lax.dot_general(a, b, (((1,), (1,)), ((), ())), preferred_element_type=F32)


def _rms_gain(o, g):
    ms = jnp.mean(o * o, axis=-1, keepdims=True)
    return (o * lax.rsqrt(ms + RMS_EPS) * g).astype(BF16)


def _swa_mask_table():
    qi = np.arange(SWA_BLOCK)[:, None]
    kj = np.arange(SWA_BLOCK)[None, :]
    prev_ok = kj >= qi
    cur_ok = np.ones((SWA_BLOCK, SWA_BLOCK), bool)
    next_ok = kj <= qi
    none = np.zeros((SWA_BLOCK, SWA_BLOCK), bool)
    kinds = [np.concatenate([none, cur_ok, next_ok], 1),
             np.concatenate([prev_ok, cur_ok, next_ok], 1),
             np.concatenate([prev_ok, cur_ok, none], 1),
             np.concatenate([none, none, none], 1)]
    return jnp.asarray(np.where(np.stack(kinds), 0.0, NEG_INF).astype(np.float32))


def _swa_kernel(sink_ref, q_ref, kp_ref, kc_ref, kn_ref, vp_ref, vc_ref, vn_ref, kx_ref, vx_ref,
                mask_ref, g_ref, o_ref, acc_ref, *, kv_heads):
    n_ctx = kx_ref.shape[0]
    mask = jnp.concatenate([mask_ref[0], jnp.zeros((SWA_BLOCK, n_ctx), F32)], axis=1)
    nk = mask.shape[1]
    for hk in range(kv_heads):
        ks = slice(hk * HEAD_DIM, (hk + 1) * HEAD_DIM)
        q = jnp.concatenate([q_ref[:, (hk * SWA_GROUP + g) * HEAD_DIM:(hk * SWA_GROUP + g + 1) * HEAD_DIM]
                             for g in range(SWA_GROUP)], axis=0)
        kb = jnp.concatenate([kp_ref[:, ks], kc_ref[:, ks], kn_ref[:, ks], kx_ref[:, ks]], axis=0)
        vb = jnp.concatenate([vp_ref[:, ks], vc_ref[:, ks], vn_ref[:, ks], vx_ref[:, ks]], axis=0)
        vb = jnp.concatenate([vb, jnp.ones((nk, HEAD_DIM), BF16)], axis=1)
        s = _dot_nt(q, kb)
        s = (s.reshape(SWA_GROUP, SWA_BLOCK, nk) + mask[None]).reshape(SWA_GROUP * SWA_BLOCK, nk)
        sink = jnp.concatenate([jnp.full((SWA_BLOCK, 1), sink_ref[hk * SWA_GROUP + g], F32)
                                for g in range(SWA_GROUP)], axis=0)
        m = jnp.maximum(jnp.max(s, axis=-1, keepdims=True), sink)
        p = jnp.exp(s - m)
        o = jnp.dot(p.astype(BF16), vb, preferred_element_type=F32)
        denom = o[:, HEAD_DIM:HEAD_DIM + 1] + jnp.exp(sink - m)
        o = o[:, :HEAD_DIM] / denom
        for g in range(SWA_GROUP):
            h = hk * SWA_GROUP + g
            acc_ref[:, h * HEAD_DIM:(h + 1) * HEAD_DIM] = o[g * SWA_BLOCK:(g + 1) * SWA_BLOCK]
    o_ref[...] = _rms_gain(acc_ref[...], g_ref[...])


def _swa(dm, u, sink, mask_tab, gain):
    nb = dm.S // SWA_BLOCK
    lat_blocks = dm.B * nb
    ctx_per = dm.L // SWA_BLOCK
    n_blocks = lat_blocks + dm.B * ctx_per
    kvw = dm.swa_kv
    k_col = dm.off_swa_k // kvw
    v_col = dm.off_swa_v // kvw
    ctx_row0 = dm.n_lat // dm.L

    def batch_of(g):
        return jnp.where(g < lat_blocks, g // nb, (g - lat_blocks) // ctx_per)

    def nbr(g, d):
        n = g % nb
        return jnp.where(g < lat_blocks, (g // nb) * nb + jnp.clip(n + d, 0, nb - 1), g)

    def kind(g):
        n = g % nb
        return jnp.where(g < lat_blocks, jnp.where(n == 0, 0, jnp.where(n == nb - 1, 2, 1)), 3)

    blk = lambda col, d: pl.BlockSpec((SWA_BLOCK, kvw), lambda g: (nbr(g, d), col))
    ctx = lambda col: pl.BlockSpec((dm.L, kvw), lambda g: (ctx_row0 + batch_of(g), col))
    in_specs = [pl.BlockSpec(memory_space=pltpu.SMEM),
                pl.BlockSpec((SWA_BLOCK, dm.swa_q), lambda g: (g, 0)),
                blk(k_col, -1), blk(k_col, 0), blk(k_col, 1),
                blk(v_col, -1), blk(v_col, 0), blk(v_col, 1),
                ctx(k_col), ctx(v_col),
                pl.BlockSpec((1, SWA_BLOCK, 3 * SWA_BLOCK), lambda g: (kind(g), 0, 0)),
                pl.BlockSpec((1, dm.swa_q), lambda g: (0, 0))]
    return pl.pallas_call(
        functools.partial(_swa_kernel, kv_heads=dm.swa_kv_heads),
        grid=(n_blocks,),
        in_specs=in_specs,
        out_specs=pl.BlockSpec((SWA_BLOCK, dm.swa_q), lambda g: (g, 0)),
        out_shape=jax.ShapeDtypeStruct((dm.rows, dm.swa_q), BF16),
        scratch_shapes=[pltpu.VMEM((SWA_BLOCK, dm.swa_q), F32)],
        compiler_params=_params(1),
        name="window_attn",
    )(sink, u, u, u, u, u, u, u, u, u, mask_tab, gain)


def _nat_bias_tables(dm, rpb_all):
    nq, nk = NAT_QROWS, 3 * NAT_QROWS
    i = np.arange(nq)[:, None]
    j = np.arange(nk)[None, :]
    row_ok = [(j >= nq) & (j < nq + NAT_ROWS) & (i >= 0),
              (j - i >= 0) & (j - i < NAT_ROWS),
              (j >= 0) & (j < NAT_ROWS) & (i >= 0)]
    dr = np.clip(j - i + (NAT_ROWS - 1) - nq, 0, 2 * NAT_ROWS - 2)
    t_row = np.zeros((3, 2 * NAT_ROWS - 1, nq, nk), np.float32)
    for k in range(3):
        for a in range(nq):
            for b in range(nk):
                if row_ok[k][a, b]:
                    t_row[k, dr[a, b], a, b] = 1.0
    cq = np.arange(GRID_W)[:, None]
    kc = np.arange(GRID_W)[None, :]
    cs = np.clip(cq - NAT_COLS // 2, 0, GRID_W - NAT_COLS)
    col_ok = (kc >= cs) & (kc < cs + NAT_COLS)
    dc = np.clip(kc - cq + NAT_COLS - 1, 0, 2 * NAT_COLS - 2)
    t_col = np.zeros((2 * NAT_COLS - 1, GRID_W, GRID_W), np.float32)
    for a in range(GRID_W):
        for b in range(GRID_W):
            if col_ok[a, b]:
                t_col[dc[a, b], a, b] = 1.0
    ok = jnp.asarray(np.einsum('trij,cqk->tiqjk', t_row, t_col) > 0.5)
    t_row, t_col = jnp.asarray(t_row), jnp.asarray(t_col)
    val = jnp.einsum('lhrc,trij,cqk->lthiqjk', rpb_all, t_row, t_col, precision=lax.Precision.HIGHEST)
    bias = jnp.where(ok[None, :, None], val.astype(BF16), jnp.asarray(NEG_INF, BF16))
    depth, _, H = bias.shape[:3]
    bias = bias.reshape(depth, 3, H, NAT_QBLOCK, 3 * NAT_QBLOCK)
    none = jnp.full((depth, 1, H, NAT_QBLOCK, 3 * NAT_QBLOCK), NEG_INF, BF16)
    return jnp.concatenate([bias, none], axis=1)


def _nat_kernel(q_ref, kp_ref, kc_ref, kn_ref, vp_ref, vc_ref, vn_ref, kx_ref, vx_ref,
                bias_ref, g_ref, o_ref, acc_ref, *, heads):
    n_ctx = kx_ref.shape[0]
    no_bias = jnp.zeros((NAT_QBLOCK, n_ctx), F32)
    nk = 3 * NAT_QBLOCK + n_ctx
    for h in range(heads):
        hs = slice(h * HEAD_DIM, (h + 1) * HEAD_DIM)
        q = q_ref[:, hs]
        kb = jnp.concatenate([kp_ref[:, hs], kc_ref[:, hs], kn_ref[:, hs], kx_ref[:, hs]], axis=0)
        vb = jnp.concatenate([vp_ref[:, hs], vc_ref[:, hs], vn_ref[:, hs], vx_ref[:, hs]], axis=0)
        vb = jnp.concatenate([vb, jnp.ones((nk, HEAD_DIM), BF16)], axis=1)
        s = _dot_nt(q, kb) + jnp.concatenate([bias_ref[0, h].astype(F32), no_bias], axis=1)
        m = jnp.max(s, axis=-1, keepdims=True)
        p = jnp.exp(s - m)
        o = jnp.dot(p.astype(BF16), vb, preferred_element_type=F32)
        acc_ref[:, hs] = o[:, :HEAD_DIM] / o[:, HEAD_DIM:HEAD_DIM + 1]
    o_ref[...] = _rms_gain(acc_ref[...], g_ref[...])


def _nat(dm, u, bias_tab, gain):
    nb = dm.S // NAT_QBLOCK
    lat_blocks = dm.B * nb
    ctx_per = dm.L // NAT_QBLOCK
    n_blocks = lat_blocks + dm.B * ctx_per
    w = dm.nat_w
    q_col, k_col, v_col = dm.off_nat_q // w, dm.off_nat_k // w, dm.off_nat_v // w
    ctx_row0 = dm.n_lat // dm.L

    def batch_of(g):
        return jnp.where(g < lat_blocks, g // nb, (g - lat_blocks) // ctx_per)

    def nbr(g, d):
        n = g % nb
        return jnp.where(g < lat_blocks, (g // nb) * nb + jnp.clip(n + d, 0, nb - 1), g)

    def kind(g):
        n = g % nb
        return jnp.where(g < lat_blocks, jnp.where(n == 0, 0, jnp.where(n == nb - 1, 2, 1)), 3)

    blk = lambda col, d: pl.BlockSpec((NAT_QBLOCK, w), lambda g: (nbr(g, d), col))
    ctx = lambda col: pl.BlockSpec((dm.L, w), lambda g: (ctx_row0 + batch_of(g), col))
    in_specs = [pl.BlockSpec((NAT_QBLOCK, w), lambda g: (g, q_col)),
                blk(k_col, -1), blk(k_col, 0), blk(k_col, 1),
                blk(v_col, -1), blk(v_col, 0), blk(v_col, 1),
                ctx(k_col), ctx(v_col),
                pl.BlockSpec((1, dm.nat_heads, NAT_QBLOCK, 3 * NAT_QBLOCK), lambda g: (kind(g), 0, 0, 0)),
                pl.BlockSpec((1, w), lambda g: (0, 0))]
    return pl.pallas_call(
        functools.partial(_nat_kernel, heads=dm.nat_heads),
        grid=(n_blocks,),
        in_specs=in_specs,
        out_specs=pl.BlockSpec((NAT_QBLOCK, w), lambda g: (g, 0)),
        out_shape=jax.ShapeDtypeStruct((dm.rows, w), BF16),
        scratch_shapes=[pltpu.VMEM((NAT_QBLOCK, w), F32)],
        compiler_params=_params(1),
        name="nbr_attn",
    )(u, u, u, u, u, u, u, u, u, bias_tab, gain)


CONV_TILE = 256
HALO = 16


def _conv_kernel(u_ref, prev_ref, next_ref, w_ref, b_ref, g_ref, o_ref, *, cw, tiles_per_seq, lat_tiles):
    i = pl.program_id(0)
    is_ctx = i >= lat_tiles
    n = i % tiles_per_seq
    has_prev = jnp.logical_and(jnp.logical_not(is_ctx), n != 0)
    has_next = jnp.logical_and(jnp.logical_not(is_ctx), n != tiles_per_seq - 1)
    x = u_ref[:, 0:cw].astype(F32)
    bg = u_ref[:, cw:2 * cw].astype(F32)
    cg = u_ref[:, 2 * cw:3 * cw].astype(F32)
    z = cg * x
    zp = (prev_ref[HALO - 1:HALO, 2 * cw:3 * cw].astype(F32) * prev_ref[HALO - 1:HALO, 0:cw].astype(F32))
    zn = (next_ref[0:1, 2 * cw:3 * cw].astype(F32) * next_ref[0:1, 0:cw].astype(F32))
    zp = jnp.where(has_prev, zp, 0.0)
    zn = jnp.where(has_next, zn, 0.0)
    row = lax.broadcasted_iota(jnp.int32, z.shape, 0)
    z_m1 = jnp.where(row == 0, zp, pltpu.roll(z, 1, 0))
    z_p1 = jnp.where(row == CONV_TILE - 1, zn, pltpu.roll(z, CONV_TILE - 1, 0))
    w = w_ref[...]
    conv = b_ref[...] + z_m1 * w[0:1] + z * w[1:2] + z_p1 * w[2:3]
    o_ref[...] = _rms_gain(bg * conv, g_ref[...])


def _conv(dm, u, w, b, gain):
    assert dm.L == CONV_TILE, "context sequences are one convolution tile"
    cw = dm.conv_w
    tiles = dm.rows // CONV_TILE
    lat_tiles = dm.n_lat // CONV_TILE
    col = dm.off_conv // (3 * cw)
    per_halo = CONV_TILE // HALO
    last_halo = dm.rows // HALO - 1
    kern = functools.partial(_conv_kernel, cw=cw, tiles_per_seq=dm.S // CONV_TILE, lat_tiles=lat_tiles)
    return pl.pallas_call(
        kern,
        grid=(tiles,),
        in_specs=[pl.BlockSpec((CONV_TILE, 3 * cw), lambda i: (i, col)),
                  pl.BlockSpec((HALO, 3 * cw), lambda i: (jnp.maximum(i * per_halo - 1, 0), col)),
                  pl.BlockSpec((HALO, 3 * cw), lambda i: (jnp.minimum((i + 1) * per_halo, last_halo), col)),
                  pl.BlockSpec((3, cw), lambda i: (0, 0)),
                  pl.BlockSpec((1, cw), lambda i: (0, 0)),
                  pl.BlockSpec((1, cw), lambda i: (0, 0))],
        out_specs=pl.BlockSpec((CONV_TILE, cw), lambda i: (i, 0)),
        out_shape=jax.ShapeDtypeStruct((dm.rows, cw), BF16),
        compiler_params=_params(1),
        name="gated_conv",
    )(u, u, u, w, b, gain)


def _layer_norm(y, g, b):
    mu = jnp.mean(y, axis=-1, keepdims=True)
    yc = y - mu
    var = jnp.mean(yc * yc, axis=-1, keepdims=True)
    return yc * lax.rsqrt(var + LN_EPS) * g + b


def _route(logits):
    lane = lax.broadcasted_iota(jnp.int32, logits.shape, 1).astype(F32)
    big = float(ROUTER_LANES)
    is_g = lane < N_GROUPS
    gl = jnp.where(is_g, logits, NEG_INF)
    gmax = jnp.max(gl, axis=-1, keepdims=True)
    ge = jnp.where(is_g, jnp.exp(gl - gmax), 0.0)
    gprob = ge / jnp.sum(ge, axis=-1, keepdims=True)
    g_p = jnp.max(gprob, axis=-1, keepdims=True)
    g_idx = jnp.min(jnp.where(jnp.logical_and(is_g, gprob == g_p), lane, big), axis=-1, keepdims=True)
    lo = N_GROUPS + g_idx * EXPERTS_PER_GROUP
    in_grp = jnp.logical_and(lane >= lo, lane < lo + EXPERTS_PER_GROUP)
    el = jnp.where(in_grp, logits, NEG_INF)
    v1 = jnp.max(el, axis=-1, keepdims=True)
    i1 = jnp.min(jnp.where(jnp.logical_and(in_grp, el == v1), lane, big), axis=-1, keepdims=True)
    el2 = jnp.where(lane == i1, NEG_INF, el)
    v2 = jnp.max(el2, axis=-1, keepdims=True)
    rest = jnp.logical_and(in_grp, lane != i1)
    i2 = jnp.min(jnp.where(jnp.logical_and(rest, el2 == v2), lane, big), axis=-1, keepdims=True)
    e2 = jnp.exp(v2 - v1)
    den = 1.0 + e2
    w1 = g_p / den
    w2 = g_p * (e2 / den)
    gates = jnp.where(lane == i1, w1, jnp.where(lane == i2, w2, 0.0))
    return jnp.where(lane == ROUTE_GROUP_LANE, g_idx, gates)


HIGH_HALF = 0xFFFF0000


def _pack_bf16_pairs(x):
    w = x.shape[1] // 2
    hi = lax.bitcast_convert_type(x[:, :w].astype(BF16).astype(F32), jnp.uint32)
    lo = lax.bitcast_convert_type(x[:, w:].astype(BF16).astype(F32), jnp.uint32)
    return hi | (lo >> 16)


def _unpack_bf16_pairs(p):
    hi = lax.bitcast_convert_type(p & jnp.uint32(HIGH_HALF), F32)
    lo = lax.bitcast_convert_type(p << 16, F32)
    return hi, lo


def _start_row_gather(idx_ref, idx0, src_hbm, dst_ref, sem, row0, n_rows):
    def issue(k, carry):
        j = row0 + k
        row = idx_ref[idx0 + j]
        pltpu.make_async_copy(src_hbm.at[pl.ds(row, 1)], dst_ref.at[pl.ds(j, 1)], sem).start()
        return carry

    lax.fori_loop(0, n_rows, issue, 0, unroll=8)


def _wait_row_gather(src_hbm, dst_ref, sem):
    pltpu.make_async_copy(src_hbm.at[pl.ds(0, dst_ref.shape[0])], dst_ref, sem).wait()


def _ln1_kernel(h_ref, mix_ref, mod_ref, g_ref, b_ref, wr_ref, wl_ref, br_ref, h_out, t_out, route_out, *,
                D, alpha):
    mod = mod_ref[0]
    y = alpha * h_ref[...] + mod[:, 2 * D:3 * D] * mix_ref[...].astype(F32)
    hn = _layer_norm(y, g_ref[...], b_ref[...])
    h_out[...] = hn
    t = hn * (1.0 + mod[:, 4 * D:5 * D]) + mod[:, 3 * D:4 * D]
    t_hi = t.astype(BF16)
    t_lo = (t - t_hi.astype(F32)).astype(BF16)
    logits = (jnp.dot(t_hi, wr_ref[...], preferred_element_type=F32)
              + jnp.dot(t_lo, wr_ref[...], preferred_element_type=F32)
              + jnp.dot(t_hi, wl_ref[...], preferred_element_type=F32)) + br_ref[...]
    route = _route(logits)
    route_out[...] = route
    t_out[:, 0:D // 2] = _pack_bf16_pairs(t)
    t_out[:, D // 2:] = lax.bitcast_convert_type(route, jnp.uint32)


def _ln1(dm, rows, h, mix, mod, g, b, wr_hi, wr_lo, br):
    tr, D = dm.tr, dm.D
    row = lambda width: pl.BlockSpec((tr, width), lambda i: (i, 0))
    vec = lambda width: pl.BlockSpec((1, width), lambda i: (0, 0))
    return pl.pallas_call(
        functools.partial(_ln1_kernel, D=D, alpha=dm.alpha),
        grid=(rows // tr,),
        in_specs=[row(D), row(D), _mod_spec(dm, tr), vec(D), vec(D),
                  pl.BlockSpec((D, ROUTER_LANES), lambda i: (0, 0)),
                  pl.BlockSpec((D, ROUTER_LANES), lambda i: (0, 0)), vec(ROUTER_LANES)],
        out_specs=[row(D), row(dm.packed_w), row(ROUTER_LANES)],
        out_shape=[jax.ShapeDtypeStruct((rows, D), F32), jax.ShapeDtypeStruct((rows, dm.packed_w), jnp.uint32),
                   jax.ShapeDtypeStruct((rows, ROUTER_LANES), F32)],
        compiler_params=_params(1),
        name="mix_residual_norm_route",
    )(h, mix, mod, g, b, wr_hi, wr_lo, br)


def _ln2_kernel(pos_ref, h_ref, f_hbm, mod_ref, modn_ref, g_ref, b_ref, h_out, a_out, fbuf, sem, *,
                D, alpha, tr, n_tiles):
    i = pl.program_id(0)
    slot = i % 2

    @pl.when(i == 0)
    def _():
        _start_row_gather(pos_ref, 0, f_hbm, fbuf.at[0], sem.at[0], 0, tr)

    @pl.when(i + 1 < n_tiles)
    def _():
        _start_row_gather(pos_ref, (i + 1) * tr, f_hbm, fbuf.at[1 - slot], sem.at[1 - slot], 0, tr)

    _wait_row_gather(f_hbm, fbuf.at[slot], sem.at[slot])
    mod = mod_ref[0]
    f_hi, f_lo = _unpack_bf16_pairs(fbuf[slot])
    ffn = jnp.concatenate([f_hi, f_lo], axis=1)
    y = alpha * h_ref[...] + mod[:, 5 * D:6 * D] * ffn
    hn = _layer_norm(y, g_ref[...], b_ref[...])
    h_out[...] = hn
    if a_out is not None:
        modn = modn_ref[0]
        a_out[...] = (hn * (1.0 + modn[:, D:2 * D]) + modn[:, 0:D]).astype(BF16)


def _ln2_last_kernel(pos_ref, h_ref, f_hbm, mod_ref, g_ref, b_ref, h_out, fbuf, sem, **kw):
    _ln2_kernel(pos_ref, h_ref, f_hbm, mod_ref, None, g_ref, b_ref, h_out, None, fbuf, sem, **kw)


def _ln2(dm, rows, pos, h, ffn_sorted, mod, mod_next, g, b):
    tr, D = dm.tr, dm.D
    n_tiles = rows // tr
    row = pl.BlockSpec((tr, D), lambda i, p: (i, 0))
    vec = pl.BlockSpec((1, D), lambda i, p: (0, 0))
    hbm = pl.BlockSpec(memory_space=pl.ANY)
    scratch = [pltpu.VMEM((2, tr, D // 2), jnp.uint32), pltpu.SemaphoreType.DMA((2,))]
    kw = dict(D=D, alpha=dm.alpha, tr=tr, n_tiles=n_tiles)
    if mod_next is None:
        return pl.pallas_call(
            functools.partial(_ln2_last_kernel, **kw),
            grid_spec=pltpu.PrefetchScalarGridSpec(
                num_scalar_prefetch=1, grid=(n_tiles,),
                in_specs=[row, hbm, _mod_spec(dm, tr), vec, vec],
                out_specs=row, scratch_shapes=scratch),
            out_shape=jax.ShapeDtypeStruct((rows, D), F32),
            compiler_params=_params(1),
            name="ffn_residual_norm_last",
        )(pos, h, ffn_sorted, mod, g, b), None
    return pl.pallas_call(
        functools.partial(_ln2_kernel, **kw),
        grid_spec=pltpu.PrefetchScalarGridSpec(
            num_scalar_prefetch=1, grid=(n_tiles,),
            in_specs=[row, hbm, _mod_spec(dm, tr), _mod_spec(dm, tr), vec, vec],
            out_specs=[row, row], scratch_shapes=scratch),
        out_shape=[jax.ShapeDtypeStruct((rows, D), F32), jax.ShapeDtypeStruct((rows, D), BF16)],
        compiler_params=_params(1),
        name="ffn_residual_norm",
    )(pos, h, ffn_sorted, mod, mod_next, g, b)


def _dispatch_plan(rows, route):
    tm = EXPERT_TILE
    n_tiles = rows // tm + N_GROUPS
    i32 = jnp.int32
    gidx = route[:, ROUTE_GROUP_LANE].astype(i32)
    onehot = (gidx[:, None] == jnp.arange(N_GROUPS, dtype=i32)[None]).astype(i32)
    csum = jnp.cumsum(onehot, axis=0)
    counts = csum[-1]
    rank = jnp.sum(onehot * csum, axis=1) - 1
    padded = (counts + tm - 1) // tm * tm
    pstart = jnp.cumsum(padded) - padded
    ustart = jnp.cumsum(counts) - counts
    total = jnp.sum(padded)
    pos = jnp.sum(onehot * pstart[None], axis=1) + rank
    order = jnp.argsort(gidx, stable=True).astype(i32)
    slot = jnp.arange(n_tiles * tm, dtype=i32)
    sgrp = jnp.sum((slot[:, None] >= pstart[None, 1:]).astype(i32), axis=1)
    shot = (sgrp[:, None] == jnp.arange(N_GROUPS, dtype=i32)[None]).astype(i32)
    k = slot - jnp.sum(shot * pstart[None], axis=1)
    valid = jnp.logical_and(k < jnp.sum(shot * counts[None], axis=1), slot < total)
    sorted_at = jnp.clip(jnp.sum(shot * ustart[None], axis=1) + k, 0, rows - 1)
    src = jnp.where(valid, order[sorted_at], slot % rows)
    tile_slot = jnp.arange(n_tiles, dtype=i32) * tm
    tgrp = jnp.where(tile_slot < total, sgrp[::tm], N_GROUPS)
    return src, pos.astype(i32), tgrp.astype(i32)


def _experts_kernel(src_ref, tgrp_ref, t_hbm, wg_ref, wu_ref, wd_ref, o_ref, tbuf, sem, tb_ref, gate_ref, acc_ref,
                    *, D, tm, n_tiles):
    i = pl.program_id(0)
    e = pl.program_id(1)
    slot = i % 2
    grp = tgrp_ref[i]
    active = grp < N_GROUPS
    half = D // 2
    part_rows = tm // EXPERTS_PER_GROUP

    @pl.when(jnp.logical_and(i == 0, e == 0))
    def _():
        _start_row_gather(src_ref, 0, t_hbm, tbuf.at[0], sem.at[0], 0, tm)

    @pl.when(e == 0)
    def _():
        _wait_row_gather(t_hbm, tbuf.at[slot], sem.at[slot])
        hi, lo = _unpack_bf16_pairs(tbuf[slot, :, 0:half])
        tb_ref[:, 0:half] = hi.astype(BF16)
        tb_ref[:, half:] = lo.astype(BF16)
        gate_ref[...] = lax.bitcast_convert_type(tbuf[slot, :, half:], F32)

    @pl.when(i + 1 < n_tiles)
    def _():
        _start_row_gather(src_ref, (i + 1) * tm, t_hbm, tbuf.at[1 - slot], sem.at[1 - slot],
                          e * part_rows, part_rows)

    @pl.when(active)
    def _():
        t = tb_ref[...]
        a = jnp.dot(t, wg_ref[...], preferred_element_type=F32)
        b = jnp.dot(t, wu_ref[...], preferred_element_type=F32)
        gates = gate_ref[...]
        lane = lax.broadcasted_iota(jnp.int32, gates.shape, 1)
        gcol = jnp.sum(jnp.where(lane == N_GROUPS + grp * EXPERTS_PER_GROUP + e, gates, 0.0),
                       axis=-1, keepdims=True)
        hid = (a * jax.nn.sigmoid(a) * b * gcol).astype(BF16)
        part = jnp.dot(hid, wd_ref[...], preferred_element_type=F32)

        @pl.when(e == 0)
        def _():
            acc_ref[...] = part

        @pl.when(e > 0)
        def _():
            acc_ref[...] += part

    last = e == EXPERTS_PER_GROUP - 1

    @pl.when(jnp.logical_and(last, active))
    def _():
        o_ref[...] = _pack_bf16_pairs(acc_ref[...])

    @pl.when(jnp.logical_and(last, jnp.logical_not(active)))
    def _():
        o_ref[...] = jnp.zeros(o_ref.shape, o_ref.dtype)


def _experts(dm, layer, src, tgrp, t_packed, wg_all, wu_all, wd_all):
    tm, D, F = EXPERT_TILE, dm.D, dm.d_expert
    n_tiles = tgrp.shape[0]

    def w_map(i, e, sr, tg):
        return (layer, jnp.minimum(tg[i], N_GROUPS - 1) * EXPERTS_PER_GROUP + e, 0, 0)

    return pl.pallas_call(
        functools.partial(_experts_kernel, D=D, tm=tm, n_tiles=n_tiles),
        grid_spec=pltpu.PrefetchScalarGridSpec(
            num_scalar_prefetch=2,
            grid=(n_tiles, EXPERTS_PER_GROUP),
            in_specs=[pl.BlockSpec(memory_space=pl.ANY),
                      pl.BlockSpec((None, None, D, F), w_map),
                      pl.BlockSpec((None, None, D, F), w_map),
                      pl.BlockSpec((None, None, F, D), w_map)],
            out_specs=pl.BlockSpec((tm, D // 2), lambda i, e, sr, tg: (i, 0)),
            scratch_shapes=[pltpu.VMEM((2, tm, dm.packed_w), jnp.uint32), pltpu.SemaphoreType.DMA((2,)),
                            pltpu.VMEM((tm, D), BF16), pltpu.VMEM((tm, ROUTER_LANES), F32),
                            pltpu.VMEM((tm, D), F32)]),
        out_shape=jax.ShapeDtypeStruct((n_tiles * tm, D // 2), jnp.uint32),
        compiler_params=_params(2),
        name="experts",
    )(src, tgrp, t_packed, wg_all, wu_all, wd_all)


def kernel(x, c, ctx, c_ctx, w_ada, b_ada, w_in, conv_w, conv_b, attn_sink, nat_rpb, mix_norm_g, w_out,
           ln1_g, ln1_b, w_router_group, b_router_group, w_router_expert, b_router_expert,
           w_gate, w_up, w_down, ln2_g, ln2_b):
    B, S, D = x.shape
    L = ctx.shape[1]
    depth = w_in.shape[0]
    dm = Dims(B, S, L, D, depth, w_gate.shape[-1])

    xin = jnp.concatenate([c, c_ctx[None], jnp.zeros((MOD_ROWS - B - 1, D), F32)], axis=0)
    mods = _ada_all(xin, w_ada, b_ada).reshape(depth, MOD_ROWS, 1, 6 * D)

    cos_t, sin_t = _rope_tables(dm)
    swa_mask = _swa_mask_table()
    nat_bias = _nat_bias_tables(dm, nat_rpb)

    w_gate_b = w_gate.astype(BF16)
    w_up_b = w_up.astype(BF16)
    w_down_b = w_down.astype(BF16)
    pad = jnp.zeros((depth, D, ROUTER_LANES - N_GROUPS - N_EXPERTS), F32)
    w_route = jnp.concatenate([w_router_group, w_router_expert, pad], axis=-1)
    w_route_hi = w_route.astype(BF16)
    w_route_lo = (w_route - w_route_hi.astype(F32)).astype(BF16)
    b_route = jnp.concatenate([b_router_group, b_router_expert, pad[:, 0]], axis=-1).reshape(depth, 1, ROUTER_LANES)

    h = jnp.concatenate([x.reshape(B * S, D), ctx.reshape(B * L, D)], axis=0)
    a = _modulate(dm, h, mods[0])
    for i in range(depth):
        last = i == depth - 1
        rows = dm.n_lat if last else dm.rows
        u = _inproj(dm, a, w_in, i, cos_t, sin_t)
        gain = mix_norm_g[i].reshape(1, D)
        y_swa = _swa(dm, u, attn_sink[i], swa_mask, gain[:, :dm.swa_q])
        y_conv = _conv(dm, u, conv_w[i], conv_b[i].reshape(1, -1), gain[:, dm.swa_q:dm.swa_q + dm.conv_w])
        y_nat = _nat(dm, u, nat_bias[i], gain[:, dm.swa_q + dm.conv_w:])
        mix = _matmul([y_swa, y_conv, y_nat], w_out, i, rows, dm.tm, 512, BF16, "out_proj")
        h_mid, t_packed, route = _ln1(dm, rows, h, mix, mods[i], ln1_g[i].reshape(1, D), ln1_b[i].reshape(1, D),
                                      w_route_hi[i], w_route_lo[i], b_route[i])
        src, pos, tgrp = _dispatch_plan(rows, route)
        ffn_sorted = _experts(dm, i, src, tgrp, t_packed, w_gate_b, w_up_b, w_down_b)
        h, a = _ln2(dm, rows, pos, h_mid, ffn_sorted, mods[i], None if last else mods[i + 1],
                    ln2_g[i].reshape(1, D), ln2_b[i].reshape(1, D))
    return h.reshape(B, S, D)
```

```python
import functools

import numpy as np
import jax
import jax.numpy as jnp
from jax import lax
from jax.experimental import pallas as pl
from jax.experimental.pallas import tpu as pltpu

F32 = jnp.float32
BF16 = jnp.bfloat16

HEAD_DIM = 128
LANES = 128
GRID_W = 64
ROPE_THETA = 10000.0
ROPE_FREQS = HEAD_DIM // 4
SWA_GROUP = 4
SWA_BLOCK = 128
NAT_ROWS = 8
NAT_COLS = 16
NAT_QROWS = 4
NAT_QBLOCK = NAT_QROWS * GRID_W
N_GROUPS = 4
EXPERTS_PER_GROUP = 4
N_EXPERTS = N_GROUPS * EXPERTS_PER_GROUP
ROUTER_LANES = 128
ROUTE_GROUP_LANE = 0
EXPERT_TILE = 512
EXPERTS_PER_STEP = 2
EXPERT_STEPS = EXPERTS_PER_GROUP // EXPERTS_PER_STEP
EXPERTS_VMEM_LIMIT = 62 * 1024 * 1024
LN_EPS = 1e-5
RMS_EPS = 1e-6
NEG_INF = -1e30
ATTN_SCALE = HEAD_DIM ** -0.5
MOD_ROWS = 8
VMEM_LIMIT = 56 * 1024 * 1024


def _params(n_axes):
    return pltpu.CompilerParams(dimension_semantics=("arbitrary",) * n_axes,
                                vmem_limit_bytes=VMEM_LIMIT)


class Dims:
    def __init__(self, B, S, L, D, depth, d_expert):
        self.B, self.S, self.L, self.D, self.depth, self.d_expert = B, S, L, D, depth, d_expert
        self.swa_q = D // 2
        self.swa_heads = self.swa_q // HEAD_DIM
        self.swa_kv_heads = self.swa_heads // SWA_GROUP
        self.swa_kv = self.swa_kv_heads * HEAD_DIM
        self.conv_w = D // 4
        self.nat_w = D // 4
        self.nat_heads = self.nat_w // HEAD_DIM
        self.off_swa_k = self.swa_q
        self.off_swa_v = self.off_swa_k + self.swa_kv
        self.off_conv = self.off_swa_v + self.swa_kv
        self.off_nat_q = self.off_conv + 3 * self.conv_w
        self.off_nat_k = self.off_nat_q + self.nat_w
        self.off_nat_v = self.off_nat_k + self.nat_w
        self.in_cols = self.off_nat_v + self.nat_w
        self.n_lat = B * S
        self.n_ctx = B * L
        self.rows = self.n_lat + self.n_ctx
        self.grid_rows = S // GRID_W
        self.packed_w = D // 2 + ROUTER_LANES
        assert self.rows % EXPERT_TILE == 0 and self.n_lat % EXPERT_TILE == 0
        self.alpha = (2.0 * depth) ** 0.25
        self.tm = 1024 if (S % 1024 == 0 and self.n_ctx % 1024 == 0) else 256
        self.tr = 256
        self.tn = self.swa_kv
        assert S % self.tm == 0 and self.n_ctx % self.tm == 0
        assert S % NAT_QBLOCK == 0 and L % NAT_QBLOCK == 0 and self.grid_rows >= 3 * NAT_QROWS
        assert S // SWA_BLOCK >= 3 and L % SWA_BLOCK == 0
        assert self.off_conv % (3 * self.conv_w) == 0
        assert B + 1 <= MOD_ROWS


def _ada_kernel(x_ref, w_ref, b_ref, o_ref):
    x = x_ref[...]
    act = x * jax.nn.sigmoid(x)
    a_hi = act.astype(BF16)
    a_lo = (act - a_hi.astype(F32)).astype(BF16)
    w = w_ref[0]
    w_hi = w.astype(BF16)
    w_lo = (w - w_hi.astype(F32)).astype(BF16)
    acc = jnp.dot(a_hi, w_hi, preferred_element_type=F32)
    acc += jnp.dot(a_lo, w_hi, preferred_element_type=F32)
    acc += jnp.dot(a_hi, w_lo, preferred_element_type=F32)
    o_ref[0] = acc + b_ref[0]


def _ada_all(xin, w_ada, b_ada):
    depth, D, n6 = w_ada.shape
    tn = 512
    return pl.pallas_call(
        _ada_kernel,
        grid=(depth, n6 // tn),
        in_specs=[pl.BlockSpec((MOD_ROWS, D), lambda l, j: (0, 0)),
                  pl.BlockSpec((1, D, tn), lambda l, j: (l, 0, j)),
                  pl.BlockSpec((1, 1, tn), lambda l, j: (l, 0, j))],
        out_specs=pl.BlockSpec((1, MOD_ROWS, tn), lambda l, j: (l, 0, j)),
        out_shape=jax.ShapeDtypeStruct((depth, MOD_ROWS, n6), F32),
        compiler_params=_params(2),
        name="ada_mod",
    )(xin, w_ada, b_ada.reshape(depth, 1, n6))


def _mod_spec(dm, tile):
    n6 = 6 * dm.D
    return pl.BlockSpec((1, 1, n6), lambda i, *_: (jnp.minimum(i * tile // dm.S, dm.B), 0, 0))


def _modulate_kernel(x_ref, c_ref, mod_ref, h_out, a_out, *, D, lat_tiles):
    mod = mod_ref[0]

    def emit(v):
        h_out[...] = v
        a_out[...] = (v * (1.0 + mod[:, D:2 * D]) + mod[:, 0:D]).astype(BF16)

    @pl.when(pl.program_id(0) < lat_tiles)
    def _():
        emit(x_ref[...])

    @pl.when(pl.program_id(0) >= lat_tiles)
    def _():
        emit(c_ref[...])


def _modulate(dm, x_rows, ctx_rows, mod):
    tr = dm.tr
    lat_tiles = dm.n_lat // tr
    row = pl.BlockSpec((tr, dm.D), lambda i: (i, 0))
    return pl.pallas_call(
        functools.partial(_modulate_kernel, D=dm.D, lat_tiles=lat_tiles),
        grid=(dm.rows // tr,),
        in_specs=[pl.BlockSpec((tr, dm.D), lambda i: (jnp.minimum(i, lat_tiles - 1), 0)),
                  pl.BlockSpec((tr, dm.D), lambda i: (jnp.maximum(i - lat_tiles, 0), 0)),
                  _mod_spec(dm, tr)],
        out_specs=[row, row],
        out_shape=[jax.ShapeDtypeStruct((dm.rows, dm.D), F32), jax.ShapeDtypeStruct((dm.rows, dm.D), BF16)],
        compiler_params=_params(1),
        name="modulate_in",
    )(x_rows, ctx_rows, mod)


def _rope_tables(dm):
    t = np.arange(dm.S)
    pos = np.stack([t // GRID_W, t % GRID_W], axis=-1).astype(np.float32)
    inv_freq = jnp.asarray(ROPE_THETA, F32) ** (-jnp.arange(ROPE_FREQS, dtype=F32) / ROPE_FREQS)
    ang = jnp.asarray(pos)[:, :, None] * inv_freq
    cos, sin = jnp.cos(ang), jnp.sin(ang)
    cos_t = jnp.concatenate([cos, cos], axis=-1).reshape(dm.S, HEAD_DIM)
    sin_t = jnp.concatenate([-sin, sin], axis=-1).reshape(dm.S, HEAD_DIM)
    cos_t = jnp.concatenate([cos_t, jnp.ones((dm.tm, HEAD_DIM), F32)], axis=0)
    sin_t = jnp.concatenate([sin_t, jnp.zeros((dm.tm, HEAD_DIM), F32)], axis=0)
    return cos_t, sin_t


def _inproj_kernel(a_ref, w_ref, cos_ref, sin_ref, o_ref, wb_ref, *, tn, n_rope_q, n_rope, nat_q0, nat_q1):
    j = pl.program_id(0)

    @pl.when(pl.program_id(1) == 0)
    def _():
        wb_ref[...] = w_ref[...].astype(BF16)

    acc = jnp.dot(a_ref[...], wb_ref[...], preferred_element_type=F32)
    is_rope = j < n_rope
    is_natq = jnp.logical_and(j >= nat_q0, j < nat_q1)

    @pl.when(is_rope)
    def _():
        cos = cos_ref[...]
        sin = sin_ref[...]
        scale = jnp.where(j < n_rope_q, ATTN_SCALE, 1.0).astype(F32)
        lane = lax.broadcasted_iota(jnp.int32, cos.shape, 1)
        first_half = (lane % (2 * ROPE_FREQS)) < ROPE_FREQS
        for c in range(tn // HEAD_DIM):
            x = acc[:, c * HEAD_DIM:(c + 1) * HEAD_DIM]
            partner = jnp.where(first_half,
                                pltpu.roll(x, HEAD_DIM - ROPE_FREQS, 1),
                                pltpu.roll(x, ROPE_FREQS, 1))
            o_ref[:, c * HEAD_DIM:(c + 1) * HEAD_DIM] = ((x * cos + partner * sin) * scale).astype(BF16)

    @pl.when(is_natq)
    def _():
        o_ref[...] = (acc * ATTN_SCALE).astype(BF16)

    @pl.when(jnp.logical_not(jnp.logical_or(is_rope, is_natq)))
    def _():
        o_ref[...] = acc.astype(BF16)


def _inproj(dm, a, w_all, layer, cos_t, sin_t):
    tm, tn = dm.tm, dm.tn
    lat_tiles = dm.n_lat // tm
    per_seq = dm.S // tm

    def tab_map(j, i):
        return (jnp.where(i < lat_tiles, i % per_seq, per_seq), 0)

    kern = functools.partial(_inproj_kernel, tn=tn, n_rope_q=dm.swa_q // tn, n_rope=dm.off_swa_v // tn,
                             nat_q0=dm.off_nat_q // tn, nat_q1=dm.off_nat_k // tn)
    return pl.pallas_call(
        kern,
        grid=(dm.in_cols // tn, dm.rows // tm),
        in_specs=[pl.BlockSpec((tm, dm.D), lambda j, i: (i, 0)),
                  pl.BlockSpec((None, dm.D, tn), lambda j, i: (layer, 0, j)),
                  pl.BlockSpec((tm, HEAD_DIM), tab_map),
                  pl.BlockSpec((tm, HEAD_DIM), tab_map)],
        out_specs=pl.BlockSpec((tm, tn), lambda j, i: (i, j)),
        out_shape=jax.ShapeDtypeStruct((dm.rows, dm.in_cols), BF16),
        scratch_shapes=[pltpu.VMEM((dm.D, tn), BF16)],
        compiler_params=_params(2),
        name="in_proj",
    )(a, w_all, cos_t, sin_t)


def _mm_kernel(*refs, k_splits):
    a_refs, w_ref, o_ref, wb_ref = refs[:-3], refs[-3], refs[-2], refs[-1]

    @pl.when(pl.program_id(1) == 0)
    def _():
        wb_ref[...] = w_ref[...].astype(BF16)

    acc = None
    k0 = 0
    for a_ref, kw in zip(a_refs, k_splits):
        part = jnp.dot(a_ref[...], wb_ref[k0:k0 + kw, :], preferred_element_type=F32)
        acc = part if acc is None else acc + part
        k0 += kw
    o_ref[...] = acc.astype(o_ref.dtype)


def _matmul(a_list, w_all, layer, rows, tm, tn, out_dtype, name):
    _, K, N = w_all.shape
    k_splits = tuple(a.shape[1] for a in a_list)
    assert sum(k_splits) == K
    in_specs = [pl.BlockSpec((tm, kw), lambda j, i: (i, 0)) for kw in k_splits]
    in_specs.append(pl.BlockSpec((None, K, tn), lambda j, i: (layer, 0, j)))
    return pl.pallas_call(
        functools.partial(_mm_kernel, k_splits=k_splits),
        grid=(N // tn, rows // tm),
        in_specs=in_specs,
        out_specs=pl.BlockSpec((tm, tn), lambda j, i: (i, j)),
        out_shape=jax.ShapeDtypeStruct((rows, N), out_dtype),
        scratch_shapes=[pltpu.VMEM((K, tn), BF16)],
        compiler_params=_params(2),
        name=name,
    )(*a_list, w_all)


def _dot_nt(a, b):
    return lax.dot_general(a, b, (((1,), (1,)), ((), ())), preferred_element_type=F32)


def _rms_gain(o, g):
    ms = jnp.mean(o * o, axis=-1, keepdims=True)
    return (o * lax.rsqrt(ms + RMS_EPS) * g).astype(BF16)


def _swa_mask_table():
    qi = np.arange(SWA_BLOCK)[:, None]
    kj = np.arange(SWA_BLOCK)[None, :]
    prev_ok = kj >= qi
    cur_ok = np.ones((SWA_BLOCK, SWA_BLOCK), bool)
    next_ok = kj <= qi
    none = np.zeros((SWA_BLOCK, SWA_BLOCK), bool)
    kinds = [np.concatenate([none, cur_ok, next_ok], 1),
             np.concatenate([prev_ok, cur_ok, next_ok], 1),
             np.concatenate([prev_ok, cur_ok, none], 1),
             np.concatenate([none, none, none], 1)]
    return jnp.asarray(np.where(np.stack(kinds), 0.0, NEG_INF).astype(np.float32))


def _swa_kernel(sink_ref, q_ref, kp_ref, kc_ref, kn_ref, vp_ref, vc_ref, vn_ref, kx_ref, vx_ref,
                mask_ref, g_ref, o_ref, acc_ref, *, kv_heads):
    n_ctx = kx_ref.shape[0]
    mask = jnp.concatenate([mask_ref[0], jnp.zeros((SWA_BLOCK, n_ctx), F32)], axis=1)
    nk = mask.shape[1]
    for hk in range(kv_heads):
        ks = slice(hk * HEAD_DIM, (hk + 1) * HEAD_DIM)
        q = jnp.concatenate([q_ref[:, (hk * SWA_GROUP + g) * HEAD_DIM:(hk * SWA_GROUP + g + 1) * HEAD_DIM]
                             for g in range(SWA_GROUP)], axis=0)
        kb = jnp.concatenate([kp_ref[:, ks], kc_ref[:, ks], kn_ref[:, ks], kx_ref[:, ks]], axis=0)
        vb = jnp.concatenate([vp_ref[:, ks], vc_ref[:, ks], vn_ref[:, ks], vx_ref[:, ks]], axis=0)
        vb = jnp.concatenate([vb, jnp.ones((nk, HEAD_DIM), BF16)], axis=1)
        s = _dot_nt(q, kb)
        s = (s.reshape(SWA_GROUP, SWA_BLOCK, nk) + mask[None]).reshape(SWA_GROUP * SWA_BLOCK, nk)
        sink = jnp.concatenate([jnp.full((SWA_BLOCK, 1), sink_ref[hk * SWA_GROUP + g], F32)
                                for g in range(SWA_GROUP)], axis=0)
        m = jnp.maximum(jnp.max(s, axis=-1, keepdims=True), sink)
        p = jnp.exp(s - m)
        o = jnp.dot(p.astype(BF16), vb, preferred_element_type=F32)
        denom = o[:, HEAD_DIM:HEAD_DIM + 1] + jnp.exp(sink - m)
        o = o[:, :HEAD_DIM] / denom
        for g in range(SWA_GROUP):
            h = hk * SWA_GROUP + g
            acc_ref[:, h * HEAD_DIM:(h + 1) * HEAD_DIM] = o[g * SWA_BLOCK:(g + 1) * SWA_BLOCK]
    o_ref[...] = _rms_gain(acc_ref[...], g_ref[...])


def _swa(dm, u, sink, mask_tab, gain):
    nb = dm.S // SWA_BLOCK
    lat_blocks = dm.B * nb
    ctx_per = dm.L // SWA_BLOCK
    n_blocks = lat_blocks + dm.B * ctx_per
    kvw = dm.swa_kv
    k_col = dm.off_swa_k // kvw
    v_col = dm.off_swa_v // kvw
    ctx_row0 = dm.n_lat // dm.L

    def batch_of(g):
        return jnp.where(g < lat_blocks, g // nb, (g - lat_blocks) // ctx_per)

    def nbr(g, d):
        n = g % nb
        return jnp.where(g < lat_blocks, (g // nb) * nb + jnp.clip(n + d, 0, nb - 1), g)

    def kind(g):
        n = g % nb
        return jnp.where(g < lat_blocks, jnp.where(n == 0, 0, jnp.where(n == nb - 1, 2, 1)), 3)

    blk = lambda col, d: pl.BlockSpec((SWA_BLOCK, kvw), lambda g: (nbr(g, d), col))
    ctx = lambda col: pl.BlockSpec((dm.L, kvw), lambda g: (ctx_row0 + batch_of(g), col))
    in_specs = [pl.BlockSpec(memory_space=pltpu.SMEM),
                pl.BlockSpec((SWA_BLOCK, dm.swa_q), lambda g: (g, 0)),
                blk(k_col, -1), blk(k_col, 0), blk(k_col, 1),
                blk(v_col, -1), blk(v_col, 0), blk(v_col, 1),
                ctx(k_col), ctx(v_col),
                pl.BlockSpec((1, SWA_BLOCK, 3 * SWA_BLOCK), lambda g: (kind(g), 0, 0)),
                pl.BlockSpec((1, dm.swa_q), lambda g: (0, 0))]
    return pl.pallas_call(
        functools.partial(_swa_kernel, kv_heads=dm.swa_kv_heads),
        grid=(n_blocks,),
        in_specs=in_specs,
        out_specs=pl.BlockSpec((SWA_BLOCK, dm.swa_q), lambda g: (g, 0)),
        out_shape=jax.ShapeDtypeStruct((dm.rows, dm.swa_q), BF16),
        scratch_shapes=[pltpu.VMEM((SWA_BLOCK, dm.swa_q), F32)],
        compiler_params=_params(1),
        name="window_attn",
    )(sink, u, u, u, u, u, u, u, u, u, mask_tab, gain)


def _nat_bias_tables(dm, rpb_all):
    nq, nk = NAT_QROWS, 3 * NAT_QROWS
    i = np.arange(nq)[:, None]
    j = np.arange(nk)[None, :]
    row_ok = [(j >= nq) & (j < nq + NAT_ROWS) & (i >= 0),
              (j - i >= 0) & (j - i < NAT_ROWS),
              (j >= 0) & (j < NAT_ROWS) & (i >= 0)]
    dr = np.clip(j - i + (NAT_ROWS - 1) - nq, 0, 2 * NAT_ROWS - 2)
    t_row = np.zeros((3, 2 * NAT_ROWS - 1, nq, nk), np.float32)
    for k in range(3):
        for a in range(nq):
            for b in range(nk):
                if row_ok[k][a, b]:
                    t_row[k, dr[a, b], a, b] = 1.0
    cq = np.arange(GRID_W)[:, None]
    kc = np.arange(GRID_W)[None, :]
    cs = np.clip(cq - NAT_COLS // 2, 0, GRID_W - NAT_COLS)
    col_ok = (kc >= cs) & (kc < cs + NAT_COLS)
    dc = np.clip(kc - cq + NAT_COLS - 1, 0, 2 * NAT_COLS - 2)
    t_col = np.zeros((2 * NAT_COLS - 1, GRID_W, GRID_W), np.float32)
    for a in range(GRID_W):
        for b in range(GRID_W):
            if col_ok[a, b]:
                t_col[dc[a, b], a, b] = 1.0
    ok = jnp.asarray(np.einsum('trij,cqk->tiqjk', t_row, t_col) > 0.5)
    t_row, t_col = jnp.asarray(t_row), jnp.asarray(t_col)
    val = jnp.einsum('lhrc,trij,cqk->lthiqjk', rpb_all, t_row, t_col, precision=lax.Precision.HIGHEST)
    bias = jnp.where(ok[None, :, None], val.astype(BF16), jnp.asarray(NEG_INF, BF16))
    depth, _, H = bias.shape[:3]
    bias = bias.reshape(depth, 3, H, NAT_QBLOCK, 3 * NAT_QBLOCK)
    none = jnp.full((depth, 1, H, NAT_QBLOCK, 3 * NAT_QBLOCK), NEG_INF, BF16)
    return jnp.concatenate([bias, none], axis=1)


def _nat_kernel(q_ref, kp_ref, kc_ref, kn_ref, vp_ref, vc_ref, vn_ref, kx_ref, vx_ref,
                bias_ref, g_ref, o_ref, acc_ref, *, heads):
    n_ctx = kx_ref.shape[0]
    no_bias = jnp.zeros((NAT_QBLOCK, n_ctx), F32)
    nk = 3 * NAT_QBLOCK + n_ctx
    for h in range(heads):
        hs = slice(h * HEAD_DIM, (h + 1) * HEAD_DIM)
        q = q_ref[:, hs]
        kb = jnp.concatenate([kp_ref[:, hs], kc_ref[:, hs], kn_ref[:, hs], kx_ref[:, hs]], axis=0)
        vb = jnp.concatenate([vp_ref[:, hs], vc_ref[:, hs], vn_ref[:, hs], vx_ref[:, hs]], axis=0)
        vb = jnp.concatenate([vb, jnp.ones((nk, HEAD_DIM), BF16)], axis=1)
        s = _dot_nt(q, kb) + jnp.concatenate([bias_ref[0, h].astype(F32), no_bias], axis=1)
        m = jnp.max(s, axis=-1, keepdims=True)
        p = jnp.exp(s - m)
        o = jnp.dot(p.astype(BF16), vb, preferred_element_type=F32)
        acc_ref[:, hs] = o[:, :HEAD_DIM] / o[:, HEAD_DIM:HEAD_DIM + 1]
    o_ref[...] = _rms_gain(acc_ref[...], g_ref[...])


def _nat(dm, u, bias_tab, gain):
    nb = dm.S // NAT_QBLOCK
    lat_blocks = dm.B * nb
    ctx_per = dm.L // NAT_QBLOCK
    n_blocks = lat_blocks + dm.B * ctx_per
    w = dm.nat_w
    q_col, k_col, v_col = dm.off_nat_q // w, dm.off_nat_k // w, dm.off_nat_v // w
    ctx_row0 = dm.n_lat // dm.L

    def batch_of(g):
        return jnp.where(g < lat_blocks, g // nb, (g - lat_blocks) // ctx_per)

    def nbr(g, d):
        n = g % nb
        return jnp.where(g < lat_blocks, (g // nb) * nb + jnp.clip(n + d, 0, nb - 1), g)

    def kind(g):
        n = g % nb
        return jnp.where(g < lat_blocks, jnp.where(n == 0, 0, jnp.where(n == nb - 1, 2, 1)), 3)

    blk = lambda col, d: pl.BlockSpec((NAT_QBLOCK, w), lambda g: (nbr(g, d), col))
    ctx = lambda col: pl.BlockSpec((dm.L, w), lambda g: (ctx_row0 + batch_of(g), col))
    in_specs = [pl.BlockSpec((NAT_QBLOCK, w), lambda g: (g, q_col)),
                blk(k_col, -1), blk(k_col, 0), blk(k_col, 1),
                blk(v_col, -1), blk(v_col, 0), blk(v_col, 1),
                ctx(k_col), ctx(v_col),
                pl.BlockSpec((1, dm.nat_heads, NAT_QBLOCK, 3 * NAT_QBLOCK), lambda g: (kind(g), 0, 0, 0)),
                pl.BlockSpec((1, w), lambda g: (0, 0))]
    return pl.pallas_call(
        functools.partial(_nat_kernel, heads=dm.nat_heads),
        grid=(n_blocks,),
        in_specs=in_specs,
        out_specs=pl.BlockSpec((NAT_QBLOCK, w), lambda g: (g, 0)),
        out_shape=jax.ShapeDtypeStruct((dm.rows, w), BF16),
        scratch_shapes=[pltpu.VMEM((NAT_QBLOCK, w), F32)],
        compiler_params=_params(1),
        name="nbr_attn",
    )(u, u, u, u, u, u, u, u, u, bias_tab, gain)


CONV_TILE = 256
HALO = 16


def _conv_kernel(u_ref, prev_ref, next_ref, w_ref, b_ref, g_ref, o_ref, *, cw, tiles_per_seq, lat_tiles):
    i = pl.program_id(0)
    is_ctx = i >= lat_tiles
    n = i % tiles_per_seq
    has_prev = jnp.logical_and(jnp.logical_not(is_ctx), n != 0)
    has_next = jnp.logical_and(jnp.logical_not(is_ctx), n != tiles_per_seq - 1)
    x = u_ref[:, 0:cw].astype(F32)
    bg = u_ref[:, cw:2 * cw].astype(F32)
    cg = u_ref[:, 2 * cw:3 * cw].astype(F32)
    z = cg * x
    zp = (prev_ref[HALO - 1:HALO, 2 * cw:3 * cw].astype(F32) * prev_ref[HALO - 1:HALO, 0:cw].astype(F32))
    zn = (next_ref[0:1, 2 * cw:3 * cw].astype(F32) * next_ref[0:1, 0:cw].astype(F32))
    zp = jnp.where(has_prev, zp, 0.0)
    zn = jnp.where(has_next, zn, 0.0)
    row = lax.broadcasted_iota(jnp.int32, z.shape, 0)
    z_m1 = jnp.where(row == 0, zp, pltpu.roll(z, 1, 0))
    z_p1 = jnp.where(row == CONV_TILE - 1, zn, pltpu.roll(z, CONV_TILE - 1, 0))
    w = w_ref[...]
    conv = b_ref[...] + z_m1 * w[0:1] + z * w[1:2] + z_p1 * w[2:3]
    o_ref[...] = _rms_gain(bg * conv, g_ref[...])


def _conv(dm, u, w, b, gain):
    assert dm.L == CONV_TILE, "context sequences are one convolution tile"
    cw = dm.conv_w
    tiles = dm.rows // CONV_TILE
    lat_tiles = dm.n_lat // CONV_TILE
    col = dm.off_conv // (3 * cw)
    per_halo = CONV_TILE // HALO
    last_halo = dm.rows // HALO - 1
    kern = functools.partial(_conv_kernel, cw=cw, tiles_per_seq=dm.S // CONV_TILE, lat_tiles=lat_tiles)
    return pl.pallas_call(
        kern,
        grid=(tiles,),
        in_specs=[pl.BlockSpec((CONV_TILE, 3 * cw), lambda i: (i, col)),
                  pl.BlockSpec((HALO, 3 * cw), lambda i: (jnp.maximum(i * per_halo - 1, 0), col)),
                  pl.BlockSpec((HALO, 3 * cw), lambda i: (jnp.minimum((i + 1) * per_halo, last_halo), col)),
                  pl.BlockSpec((3, cw), lambda i: (0, 0)),
                  pl.BlockSpec((1, cw), lambda i: (0, 0)),
                  pl.BlockSpec((1, cw), lambda i: (0, 0))],
        out_specs=pl.BlockSpec((CONV_TILE, cw), lambda i: (i, 0)),
        out_shape=jax.ShapeDtypeStruct((dm.rows, cw), BF16),
        compiler_params=_params(1),
        name="gated_conv",
    )(u, u, u, w, b, gain)


def _layer_norm(y, g, b):
    mu = jnp.mean(y, axis=-1, keepdims=True)
    yc = y - mu
    var = jnp.mean(yc * yc, axis=-1, keepdims=True)
    return yc * lax.rsqrt(var + LN_EPS) * g + b


def _route(logits):
    lane = lax.broadcasted_iota(jnp.int32, logits.shape, 1).astype(F32)
    big = float(ROUTER_LANES)
    is_g = lane < N_GROUPS
    gl = jnp.where(is_g, logits, NEG_INF)
    gmax = jnp.max(gl, axis=-1, keepdims=True)
    ge = jnp.where(is_g, jnp.exp(gl - gmax), 0.0)
    gprob = ge / jnp.sum(ge, axis=-1, keepdims=True)
    g_p = jnp.max(gprob, axis=-1, keepdims=True)
    g_idx = jnp.min(jnp.where(jnp.logical_and(is_g, gprob == g_p), lane, big), axis=-1, keepdims=True)
    lo = N_GROUPS + g_idx * EXPERTS_PER_GROUP
    in_grp = jnp.logical_and(lane >= lo, lane < lo + EXPERTS_PER_GROUP)
    el = jnp.where(in_grp, logits, NEG_INF)
    v1 = jnp.max(el, axis=-1, keepdims=True)
    i1 = jnp.min(jnp.where(jnp.logical_and(in_grp, el == v1), lane, big), axis=-1, keepdims=True)
    el2 = jnp.where(lane == i1, NEG_INF, el)
    v2 = jnp.max(el2, axis=-1, keepdims=True)
    rest = jnp.logical_and(in_grp, lane != i1)
    i2 = jnp.min(jnp.where(jnp.logical_and(rest, el2 == v2), lane, big), axis=-1, keepdims=True)
    e2 = jnp.exp(v2 - v1)
    den = 1.0 + e2
    w1 = g_p / den
    w2 = g_p * (e2 / den)
    gates = jnp.where(lane == i1, w1, jnp.where(lane == i2, w2, 0.0))
    return jnp.where(lane == ROUTE_GROUP_LANE, g_idx, gates)


HIGH_HALF = 0xFFFF0000


def _pack_bf16_pairs(x):
    w = x.shape[1] // 2
    hi = lax.bitcast_convert_type(x[:, :w].astype(BF16).astype(F32), jnp.uint32)
    lo = lax.bitcast_convert_type(x[:, w:].astype(BF16).astype(F32), jnp.uint32)
    return hi | (lo >> 16)


def _unpack_bf16_pairs(p):
    hi = lax.bitcast_convert_type(p & jnp.uint32(HIGH_HALF), F32)
    lo = lax.bitcast_convert_type(p << 16, F32)
    return hi, lo


def _start_row_gather(idx_ref, idx0, src_hbm, dst_ref, sem, row0, n_rows):
    def issue(k, carry):
        j = row0 + k
        row = idx_ref[idx0 + j]
        pltpu.make_async_copy(src_hbm.at[pl.ds(row, 1)], dst_ref.at[pl.ds(j, 1)], sem).start()
        return carry

    lax.fori_loop(0, n_rows, issue, 0, unroll=8)


def _wait_row_gather(src_hbm, dst_ref, sem):
    pltpu.make_async_copy(src_hbm.at[pl.ds(0, dst_ref.shape[0])], dst_ref, sem).wait()


def _ln1_kernel(h_ref, mix_ref, mod_ref, g_ref, b_ref, wr_ref, wl_ref, br_ref, h_out, t_out, route_out, *,
                D, alpha):
    mod = mod_ref[0]
    y = alpha * h_ref[...] + mod[:, 2 * D:3 * D] * mix_ref[...].astype(F32)
    hn = _layer_norm(y, g_ref[...], b_ref[...])
    h_out[...] = hn
    t = hn * (1.0 + mod[:, 4 * D:5 * D]) + mod[:, 3 * D:4 * D]
    t_hi = t.astype(BF16)
    t_lo = (t - t_hi.astype(F32)).astype(BF16)
    logits = (jnp.dot(t_hi, wr_ref[...], preferred_element_type=F32)
              + jnp.dot(t_lo, wr_ref[...], preferred_element_type=F32)
              + jnp.dot(t_hi, wl_ref[...], preferred_element_type=F32)) + br_ref[...]
    route = _route(logits)
    route_out[...] = route
    t_out[:, 0:D // 2] = _pack_bf16_pairs(t)
    t_out[:, D // 2:] = lax.bitcast_convert_type(route, jnp.uint32)


def _ln1(dm, rows, h, mix, mod, g, b, wr_hi, wr_lo, br):
    tr, D = dm.tr, dm.D
    row = lambda width: pl.BlockSpec((tr, width), lambda i: (i, 0))
    vec = lambda width: pl.BlockSpec((1, width), lambda i: (0, 0))
    return pl.pallas_call(
        functools.partial(_ln1_kernel, D=D, alpha=dm.alpha),
        grid=(rows // tr,),
        in_specs=[row(D), row(D), _mod_spec(dm, tr), vec(D), vec(D),
                  pl.BlockSpec((D, ROUTER_LANES), lambda i: (0, 0)),
                  pl.BlockSpec((D, ROUTER_LANES), lambda i: (0, 0)), vec(ROUTER_LANES)],
        out_specs=[row(D), row(dm.packed_w), row(ROUTER_LANES)],
        out_shape=[jax.ShapeDtypeStruct((rows, D), F32), jax.ShapeDtypeStruct((rows, dm.packed_w), jnp.uint32),
                   jax.ShapeDtypeStruct((rows, ROUTER_LANES), F32)],
        compiler_params=_params(1),
        name="mix_residual_norm_route",
    )(h, mix, mod, g, b, wr_hi, wr_lo, br)


def _ln2_kernel(pos_ref, h_ref, f_hbm, mod_ref, modn_ref, g_ref, b_ref, h_out, a_out, fbuf, sem, *,
                D, alpha, tr, n_tiles):
    i = pl.program_id(0)
    slot = i % 2

    @pl.when(i == 0)
    def _():
        _start_row_gather(pos_ref, 0, f_hbm, fbuf.at[0], sem.at[0], 0, tr)

    @pl.when(i + 1 < n_tiles)
    def _():
        _start_row_gather(pos_ref, (i + 1) * tr, f_hbm, fbuf.at[1 - slot], sem.at[1 - slot], 0, tr)

    _wait_row_gather(f_hbm, fbuf.at[slot], sem.at[slot])
    mod = mod_ref[0]
    f_hi, f_lo = _unpack_bf16_pairs(fbuf[slot])
    ffn = jnp.concatenate([f_hi, f_lo], axis=1)
    y = alpha * h_ref[...] + mod[:, 5 * D:6 * D] * ffn
    hn = _layer_norm(y, g_ref[...], b_ref[...])
    h_out[...] = hn
    if a_out is not None:
        modn = modn_ref[0]
        a_out[...] = (hn * (1.0 + modn[:, D:2 * D]) + modn[:, 0:D]).astype(BF16)


def _ln2_last_kernel(pos_ref, h_ref, f_hbm, mod_ref, g_ref, b_ref, h_out, fbuf, sem, **kw):
    _ln2_kernel(pos_ref, h_ref, f_hbm, mod_ref, None, g_ref, b_ref, h_out, None, fbuf, sem, **kw)


def _ln2(dm, rows, pos, h, ffn_sorted, mod, mod_next, g, b):
    tr, D = dm.tr, dm.D
    n_tiles = rows // tr
    row = pl.BlockSpec((tr, D), lambda i, p: (i, 0))
    vec = pl.BlockSpec((1, D), lambda i, p: (0, 0))
    hbm = pl.BlockSpec(memory_space=pl.ANY)
    scratch = [pltpu.VMEM((2, tr, D // 2), jnp.uint32), pltpu.SemaphoreType.DMA((2,))]
    kw = dict(D=D, alpha=dm.alpha, tr=tr, n_tiles=n_tiles)
    if mod_next is None:
        return pl.pallas_call(
            functools.partial(_ln2_last_kernel, **kw),
            grid_spec=pltpu.PrefetchScalarGridSpec(
                num_scalar_prefetch=1, grid=(n_tiles,),
                in_specs=[row, hbm, _mod_spec(dm, tr), vec, vec],
                out_specs=row, scratch_shapes=scratch),
            out_shape=jax.ShapeDtypeStruct((rows, D), F32),
            compiler_params=_params(1),
            name="ffn_residual_norm_last",
        )(pos, h, ffn_sorted, mod, g, b), None
    return pl.pallas_call(
        functools.partial(_ln2_kernel, **kw),
        grid_spec=pltpu.PrefetchScalarGridSpec(
            num_scalar_prefetch=1, grid=(n_tiles,),
            in_specs=[row, hbm, _mod_spec(dm, tr), _mod_spec(dm, tr), vec, vec],
            out_specs=[row, row], scratch_shapes=scratch),
        out_shape=[jax.ShapeDtypeStruct((rows, D), F32), jax.ShapeDtypeStruct((rows, D), BF16)],
        compiler_params=_params(1),
        name="ffn_residual_norm",
    )(pos, h, ffn_sorted, mod, mod_next, g, b)


def _dispatch_plan(rows, route):
    tm = EXPERT_TILE
    n_tiles = rows // tm + N_GROUPS
    i32 = jnp.int32
    gidx = route[:, ROUTE_GROUP_LANE].astype(i32)
    onehot = (gidx[:, None] == jnp.arange(N_GROUPS, dtype=i32)[None]).astype(i32)
    csum = jnp.cumsum(onehot, axis=0)
    counts = csum[-1]
    rank = jnp.sum(onehot * csum, axis=1) - 1
    padded = (counts + tm - 1) // tm * tm
    pstart = jnp.cumsum(padded) - padded
    ustart = jnp.cumsum(counts) - counts
    total = jnp.sum(padded)
    pos = jnp.sum(onehot * pstart[None], axis=1) + rank
    order = jnp.argsort(gidx, stable=True).astype(i32)
    slot = jnp.arange(n_tiles * tm, dtype=i32)
    sgrp = jnp.sum((slot[:, None] >= pstart[None, 1:]).astype(i32), axis=1)
    shot = (sgrp[:, None] == jnp.arange(N_GROUPS, dtype=i32)[None]).astype(i32)
    k = slot - jnp.sum(shot * pstart[None], axis=1)
    valid = jnp.logical_and(k < jnp.sum(shot * counts[None], axis=1), slot < total)
    sorted_at = jnp.clip(jnp.sum(shot * ustart[None], axis=1) + k, 0, rows - 1)
    src = jnp.where(valid, order[sorted_at], slot % rows)
    tile_slot = jnp.arange(n_tiles, dtype=i32) * tm
    tgrp = jnp.where(tile_slot < total, sgrp[::tm], N_GROUPS)
    return src, pos.astype(i32), tgrp.astype(i32)


def _experts_kernel(src_ref, tgrp_ref, t_hbm, wg_ref, wu_ref, wd_ref, o_ref, tbuf, sem, tb_ref, gate_ref, acc_ref,
                    *, D, tm, n_tiles):
    i = pl.program_id(0)
    e = pl.program_id(1)
    slot = i % 2
    grp = tgrp_ref[i]
    active = grp < N_GROUPS
    half = D // 2
    part_rows = tm // EXPERT_STEPS
    F = wg_ref.shape[-1]

    @pl.when(jnp.logical_and(i == 0, e == 0))
    def _():
        _start_row_gather(src_ref, 0, t_hbm, tbuf.at[0], sem.at[0], 0, tm)

    @pl.when(e == 0)
    def _():
        _wait_row_gather(t_hbm, tbuf.at[slot], sem.at[slot])
        hi, lo = _unpack_bf16_pairs(tbuf[slot, :, 0:half])
        tb_ref[:, 0:half] = hi.astype(BF16)
        tb_ref[:, half:] = lo.astype(BF16)
        gate_ref[...] = lax.bitcast_convert_type(tbuf[slot, :, half:], F32)

    has_next = i + 1 < n_tiles

    @pl.when(jnp.logical_and(has_next, jnp.logical_not(active)))
    def _():
        _start_row_gather(src_ref, (i + 1) * tm, t_hbm, tbuf.at[1 - slot], sem.at[1 - slot],
                          e * part_rows, part_rows)

    @pl.when(active)
    def _():
        nxt = jnp.minimum(i + 1, n_tiles - 1) * tm + e * part_rows
        for k in range(part_rows):
            pltpu.make_async_copy(t_hbm.at[pl.ds(src_ref[nxt + k], 1)],
                                  tbuf.at[1 - slot, pl.ds(e * part_rows + k, 1)], sem.at[1 - slot]).start()
        t = tb_ref[...]
        gates = gate_ref[...]
        lane = lax.broadcasted_iota(jnp.int32, gates.shape, 1)
        hids = []
        for k in range(EXPERTS_PER_STEP):
            a = jnp.dot(t, wg_ref[k], preferred_element_type=F32)
            b = jnp.dot(t, wu_ref[k], preferred_element_type=F32)
            expert_lane = N_GROUPS + grp * EXPERTS_PER_GROUP + e * EXPERTS_PER_STEP + k
            gcol = jnp.sum(jnp.where(lane == expert_lane, gates, 0.0), axis=-1, keepdims=True)
            hids.append((a * jax.nn.sigmoid(a) * b * gcol).astype(BF16))
        hid = jnp.concatenate(hids, axis=1)
        part = jnp.dot(hid, wd_ref[...].reshape(EXPERTS_PER_STEP * F, D), preferred_element_type=F32)

        @pl.when(e == 0)
        def _():
            acc_ref[...] = part

        @pl.when(e == EXPERT_STEPS - 1)
        def _():
            o_ref[...] = _pack_bf16_pairs(acc_ref[...] + part)

    @pl.when(jnp.logical_and(e == EXPERT_STEPS - 1, jnp.logical_not(active)))
    def _():
        o_ref[...] = jnp.zeros(o_ref.shape, o_ref.dtype)

    @pl.when(jnp.logical_and(active, jnp.logical_and(jnp.logical_not(has_next), e == EXPERT_STEPS - 1)))
    def _():
        _wait_row_gather(t_hbm, tbuf.at[1 - slot], sem.at[1 - slot])


def _experts(dm, layer, src, tgrp, t_packed, wg_all, wu_all, wd_all):
    tm, D, F = EXPERT_TILE, dm.D, dm.d_expert
    n_tiles = tgrp.shape[0]

    def w_map(i, e, sr, tg):
        return (layer, jnp.minimum(tg[i], N_GROUPS - 1) * EXPERT_STEPS + e, 0, 0)

    return pl.pallas_call(
        functools.partial(_experts_kernel, D=D, tm=tm, n_tiles=n_tiles),
        grid_spec=pltpu.PrefetchScalarGridSpec(
            num_scalar_prefetch=2,
            grid=(n_tiles, EXPERT_STEPS),
            in_specs=[pl.BlockSpec(memory_space=pl.ANY),
                      pl.BlockSpec((None, EXPERTS_PER_STEP, D, F), w_map),
                      pl.BlockSpec((None, EXPERTS_PER_STEP, D, F), w_map),
                      pl.BlockSpec((None, EXPERTS_PER_STEP, F, D), w_map)],
            out_specs=pl.BlockSpec((tm, D // 2), lambda i, e, sr, tg: (i, 0)),
            scratch_shapes=[pltpu.VMEM((2, tm, dm.packed_w), jnp.uint32), pltpu.SemaphoreType.DMA((2,)),
                            pltpu.VMEM((tm, D), BF16), pltpu.VMEM((tm, ROUTER_LANES), F32),
                            pltpu.VMEM((tm, D), F32)]),
        out_shape=jax.ShapeDtypeStruct((n_tiles * tm, D // 2), jnp.uint32),
        compiler_params=pltpu.CompilerParams(dimension_semantics=("arbitrary", "arbitrary"),
                                             vmem_limit_bytes=EXPERTS_VMEM_LIMIT),
        name="experts",
    )(src, tgrp, t_packed, wg_all, wu_all, wd_all)


def kernel(x, c, ctx, c_ctx, w_ada, b_ada, w_in, conv_w, conv_b, attn_sink, nat_rpb, mix_norm_g, w_out,
           ln1_g, ln1_b, w_router_group, b_router_group, w_router_expert, b_router_expert,
           w_gate, w_up, w_down, ln2_g, ln2_b):
    B, S, D = x.shape
    L = ctx.shape[1]
    depth = w_in.shape[0]
    dm = Dims(B, S, L, D, depth, w_gate.shape[-1])

    xin = jnp.concatenate([c, c_ctx[None], jnp.zeros((MOD_ROWS - B - 1, D), F32)], axis=0)
    mods = _ada_all(xin, w_ada, b_ada).reshape(depth, MOD_ROWS, 1, 6 * D)

    cos_t, sin_t = _rope_tables(dm)
    swa_mask = _swa_mask_table()
    nat_bias = _nat_bias_tables(dm, nat_rpb)

    w_gate_b = w_gate.astype(BF16)
    w_up_b = w_up.astype(BF16)
    w_down_b = w_down.astype(BF16)
    pad = jnp.zeros((depth, D, ROUTER_LANES - N_GROUPS - N_EXPERTS), F32)
    w_route = jnp.concatenate([w_router_group, w_router_expert, pad], axis=-1)
    w_route_hi = w_route.astype(BF16)
    w_route_lo = (w_route - w_route_hi.astype(F32)).astype(BF16)
    b_route = jnp.concatenate([b_router_group, b_router_expert, pad[:, 0]], axis=-1).reshape(depth, 1, ROUTER_LANES)

    h, a = _modulate(dm, x.reshape(B * S, D), ctx.reshape(B * L, D), mods[0])
    for i in range(depth):
        last = i == depth - 1
        rows = dm.n_lat if last else dm.rows
        u = _inproj(dm, a, w_in, i, cos_t, sin_t)
        gain = mix_norm_g[i].reshape(1, D)
        y_swa = _swa(dm, u, attn_sink[i], swa_mask, gain[:, :dm.swa_q])
        y_conv = _conv(dm, u, conv_w[i], conv_b[i].reshape(1, -1), gain[:, dm.swa_q:dm.swa_q + dm.conv_w])
        y_nat = _nat(dm, u, nat_bias[i], gain[:, dm.swa_q + dm.conv_w:])
        mix = _matmul([y_swa, y_conv, y_nat], w_out, i, rows, dm.tm, 512, BF16, "out_proj")
        h_mid, t_packed, route = _ln1(dm, rows, h, mix, mods[i], ln1_g[i].reshape(1, D), ln1_b[i].reshape(1, D),
                                      w_route_hi[i], w_route_lo[i], b_route[i])
        src, pos, tgrp = _dispatch_plan(rows, route)
        ffn_sorted = _experts(dm, i, src, tgrp, t_packed, w_gate_b, w_up_b, w_down_b)
        h, a = _ln2(dm, rows, pos, h_mid, ffn_sorted, mods[i], None if last else mods[i + 1],
                    ln2_g[i].reshape(1, D), ln2_b[i].reshape(1, D))
    return h.reshape(B, S, D)
```

```python
import functools

import numpy as np
import jax
import jax.numpy as jnp
from jax import lax
from jax.experimental import pallas as pl
from jax.experimental.pallas import tpu as pltpu

F32 = jnp.float32
BF16 = jnp.bfloat16

HEAD_DIM = 128
LANES = 128
GRID_W = 64
ROPE_THETA = 10000.0
ROPE_FREQS = HEAD_DIM // 4
SWA_GROUP = 4
SWA_BLOCK = 128
NAT_ROWS = 8
NAT_COLS = 16
NAT_QROWS = 4
NAT_QBLOCK = NAT_QROWS * GRID_W
N_GROUPS = 4
EXPERTS_PER_GROUP = 4
N_EXPERTS = N_GROUPS * EXPERTS_PER_GROUP
ROUTER_LANES = 128
ROUTE_GROUP_LANE = 0
EXPERT_TILE = 512
EXPERTS_PER_STEP = 2
EXPERT_STEPS = EXPERTS_PER_GROUP // EXPERTS_PER_STEP
EXPERTS_VMEM_LIMIT = 62 * 1024 * 1024
LN_EPS = 1e-5
RMS_EPS = 1e-6
NEG_INF = -1e30
ATTN_SCALE = HEAD_DIM ** -0.5
MOD_ROWS = 8
VMEM_LIMIT = 56 * 1024 * 1024


def _params(n_axes):
    return pltpu.CompilerParams(dimension_semantics=("arbitrary",) * n_axes,
                                vmem_limit_bytes=VMEM_LIMIT)


class Dims:
    def __init__(self, B, S, L, D, depth, d_expert):
        self.B, self.S, self.L, self.D, self.depth, self.d_expert = B, S, L, D, depth, d_expert
        self.swa_q = D // 2
        self.swa_heads = self.swa_q // HEAD_DIM
        self.swa_kv_heads = self.swa_heads // SWA_GROUP
        self.swa_kv = self.swa_kv_heads * HEAD_DIM
        self.conv_w = D // 4
        self.nat_w = D // 4
        self.nat_heads = self.nat_w // HEAD_DIM
        self.off_swa_k = self.swa_q
        self.off_swa_v = self.off_swa_k + self.swa_kv
        self.off_conv = self.off_swa_v + self.swa_kv
        self.off_nat_q = self.off_conv + 3 * self.conv_w
        self.off_nat_k = self.off_nat_q + self.nat_w
        self.off_nat_v = self.off_nat_k + self.nat_w
        self.in_cols = self.off_nat_v + self.nat_w
        self.rest_nat_q = 3 * self.conv_w
        self.rest_nat_k = self.rest_nat_q + self.nat_w
        self.rest_nat_v = self.rest_nat_k + self.nat_w
        self.rest_swa_v = self.rest_nat_v + self.nat_w
        self.n_lat = B * S
        self.n_ctx = B * L
        self.rows = self.n_lat + self.n_ctx
        self.grid_rows = S // GRID_W
        self.packed_w = D // 2 + ROUTER_LANES
        assert self.rows % EXPERT_TILE == 0 and self.n_lat % EXPERT_TILE == 0
        self.alpha = (2.0 * depth) ** 0.25
        self.tm = 1024 if (S % 1024 == 0 and self.n_ctx % 1024 == 0) else 256
        self.tr = 256
        self.tn = self.swa_kv
        assert S % self.tm == 0 and self.n_ctx % self.tm == 0
        assert S % NAT_QBLOCK == 0 and L % NAT_QBLOCK == 0 and self.grid_rows >= 3 * NAT_QROWS
        assert S // SWA_BLOCK >= 3 and L % SWA_BLOCK == 0
        assert self.off_conv % (3 * self.conv_w) == 0
        assert B + 1 <= MOD_ROWS


def _ada_kernel(x_ref, w_ref, b_ref, o_ref):
    x = x_ref[...]
    act = x * jax.nn.sigmoid(x)
    a_hi = act.astype(BF16)
    a_lo = (act - a_hi.astype(F32)).astype(BF16)
    w = w_ref[0]
    w_hi = w.astype(BF16)
    w_lo = (w - w_hi.astype(F32)).astype(BF16)
    acc = jnp.dot(a_hi, w_hi, preferred_element_type=F32)
    acc += jnp.dot(a_lo, w_hi, preferred_element_type=F32)
    acc += jnp.dot(a_hi, w_lo, preferred_element_type=F32)
    o_ref[0] = acc + b_ref[0]


def _ada_all(xin, w_ada, b_ada):
    depth, D, n6 = w_ada.shape
    tn = 512
    return pl.pallas_call(
        _ada_kernel,
        grid=(depth, n6 // tn),
        in_specs=[pl.BlockSpec((MOD_ROWS, D), lambda l, j: (0, 0)),
                  pl.BlockSpec((1, D, tn), lambda l, j: (l, 0, j)),
                  pl.BlockSpec((1, 1, tn), lambda l, j: (l, 0, j))],
        out_specs=pl.BlockSpec((1, MOD_ROWS, tn), lambda l, j: (l, 0, j)),
        out_shape=jax.ShapeDtypeStruct((depth, MOD_ROWS, n6), F32),
        compiler_params=_params(2),
        name="ada_mod",
    )(xin, w_ada, b_ada.reshape(depth, 1, n6))


def _mod_spec(dm, tile):
    n6 = 6 * dm.D
    return pl.BlockSpec((1, 1, n6), lambda i, *_: (jnp.minimum(i * tile // dm.S, dm.B), 0, 0))


def _modulate_kernel(x_ref, c_ref, mod_ref, h_out, a_out, *, D, lat_tiles):
    mod = mod_ref[0]

    def emit(v):
        h_out[...] = v
        a_out[...] = (v * (1.0 + mod[:, D:2 * D]) + mod[:, 0:D]).astype(BF16)

    @pl.when(pl.program_id(0) < lat_tiles)
    def _():
        emit(x_ref[...])

    @pl.when(pl.program_id(0) >= lat_tiles)
    def _():
        emit(c_ref[...])


def _modulate(dm, x_rows, ctx_rows, mod):
    tr = dm.tr
    lat_tiles = dm.n_lat // tr
    row = pl.BlockSpec((tr, dm.D), lambda i: (i, 0))
    return pl.pallas_call(
        functools.partial(_modulate_kernel, D=dm.D, lat_tiles=lat_tiles),
        grid=(dm.rows // tr,),
        in_specs=[pl.BlockSpec((tr, dm.D), lambda i: (jnp.minimum(i, lat_tiles - 1), 0)),
                  pl.BlockSpec((tr, dm.D), lambda i: (jnp.maximum(i - lat_tiles, 0), 0)),
                  _mod_spec(dm, tr)],
        out_specs=[row, row],
        out_shape=[jax.ShapeDtypeStruct((dm.rows, dm.D), F32), jax.ShapeDtypeStruct((dm.rows, dm.D), BF16)],
        compiler_params=_params(1),
        name="modulate_in",
    )(x_rows, ctx_rows, mod)


def _rope_tables(dm):
    t = np.arange(dm.S)
    pos = np.stack([t // GRID_W, t % GRID_W], axis=-1).astype(np.float32)
    inv_freq = jnp.asarray(ROPE_THETA, F32) ** (-jnp.arange(ROPE_FREQS, dtype=F32) / ROPE_FREQS)
    ang = jnp.asarray(pos)[:, :, None] * inv_freq
    cos, sin = jnp.cos(ang), jnp.sin(ang)
    cos_t = jnp.concatenate([cos, cos], axis=-1).reshape(dm.S, HEAD_DIM)
    sin_t = jnp.concatenate([-sin, sin], axis=-1).reshape(dm.S, HEAD_DIM)
    cos_t = jnp.concatenate([cos_t, jnp.ones((dm.tm, HEAD_DIM), F32)], axis=0)
    sin_t = jnp.concatenate([sin_t, jnp.zeros((dm.tm, HEAD_DIM), F32)], axis=0)
    return cos_t, sin_t


def _resident_weight_dot(a_ref, w_ref, wb_ref):
    @pl.when(pl.program_id(1) == 0)
    def _():
        wb_ref[...] = w_ref[...].astype(BF16)

    return jnp.dot(a_ref[...], wb_ref[...], preferred_element_type=F32)


def _inproj_rope_kernel(a_ref, w_ref, cos_ref, sin_ref, o_ref, wb_ref, *, tn, n_q):
    acc = _resident_weight_dot(a_ref, w_ref, wb_ref)
    cos = cos_ref[...]
    sin = sin_ref[...]
    scale = jnp.where(pl.program_id(0) < n_q, ATTN_SCALE, 1.0).astype(F32)
    lane = lax.broadcasted_iota(jnp.int32, cos.shape, 1)
    first_half = (lane % (2 * ROPE_FREQS)) < ROPE_FREQS
    for c in range(tn // HEAD_DIM):
        x = acc[:, c * HEAD_DIM:(c + 1) * HEAD_DIM]
        partner = jnp.where(first_half,
                            pltpu.roll(x, HEAD_DIM - ROPE_FREQS, 1),
                            pltpu.roll(x, ROPE_FREQS, 1))
        o_ref[:, c * HEAD_DIM:(c + 1) * HEAD_DIM] = ((x * cos + partner * sin) * scale).astype(BF16)


def _inproj_plain_kernel(a_ref, w_ref, o_ref, wb_ref, *, natq0, natq1):
    j = pl.program_id(0)
    acc = _resident_weight_dot(a_ref, w_ref, wb_ref)
    o_ref[...] = acc.astype(BF16)

    @pl.when(jnp.logical_and(j >= natq0, j < natq1))
    def _():
        o_ref[...] = (acc * ATTN_SCALE).astype(BF16)


def _inproj(dm, a, w_all, layer, cos_t, sin_t):
    tm, tn = dm.tm, dm.tn
    lat_tiles = dm.n_lat // tm
    per_seq = dm.S // tm
    row_tiles = dm.rows // tm

    def tab_map(j, i):
        return (jnp.where(i < lat_tiles, i % per_seq, per_seq), 0)

    a_spec = pl.BlockSpec((tm, dm.D), lambda j, i: (i, 0))
    o_spec = pl.BlockSpec((tm, tn), lambda j, i: (i, j))
    scratch = [pltpu.VMEM((dm.D, tn), BF16)]
    n_rot = dm.off_swa_v // tn
    u_rot = pl.pallas_call(
        functools.partial(_inproj_rope_kernel, tn=tn, n_q=dm.swa_q // tn),
        grid=(n_rot, row_tiles),
        in_specs=[a_spec,
                  pl.BlockSpec((None, dm.D, tn), lambda j, i: (layer, 0, j)),
                  pl.BlockSpec((tm, HEAD_DIM), tab_map),
                  pl.BlockSpec((tm, HEAD_DIM), tab_map)],
        out_specs=o_spec,
        out_shape=jax.ShapeDtypeStruct((dm.rows, dm.off_swa_v), BF16),
        scratch_shapes=scratch,
        compiler_params=_params(2),
        name="in_proj_rotary",
    )(a, w_all, cos_t, sin_t)

    n_rest = (dm.in_cols - dm.off_swa_v) // tn
    conv_tile0 = dm.off_conv // tn
    swa_v_tile = dm.off_swa_v // tn

    def w_map(j, i):
        return (layer, 0, jnp.where(j < n_rest - 1, j + conv_tile0, swa_v_tile))

    u_rest = pl.pallas_call(
        functools.partial(_inproj_plain_kernel, natq0=dm.rest_nat_q // tn, natq1=dm.rest_nat_k // tn),
        grid=(n_rest, row_tiles),
        in_specs=[a_spec, pl.BlockSpec((None, dm.D, tn), w_map)],
        out_specs=o_spec,
        out_shape=jax.ShapeDtypeStruct((dm.rows, dm.in_cols - dm.off_swa_v), BF16),
        scratch_shapes=scratch,
        compiler_params=_params(2),
        name="in_proj_rest",
    )(a, w_all)
    return u_rot, u_rest


def _mm_kernel(*refs, k_splits):
    a_refs, w_ref, o_ref, wb_ref = refs[:-3], refs[-3], refs[-2], refs[-1]

    @pl.when(pl.program_id(1) == 0)
    def _():
        wb_ref[...] = w_ref[...].astype(BF16)

    acc = None
    k0 = 0
    for a_ref, kw in zip(a_refs, k_splits):
        part = jnp.dot(a_ref[...], wb_ref[k0:k0 + kw, :], preferred_element_type=F32)
        acc = part if acc is None else acc + part
        k0 += kw
    o_ref[...] = acc.astype(o_ref.dtype)


def _matmul(a_list, w_all, layer, rows, tm, tn, out_dtype, name):
    _, K, N = w_all.shape
    k_splits = tuple(a.shape[1] for a in a_list)
    assert sum(k_splits) == K
    in_specs = [pl.BlockSpec((tm, kw), lambda j, i: (i, 0)) for kw in k_splits]
    in_specs.append(pl.BlockSpec((None, K, tn), lambda j, i: (layer, 0, j)))
    return pl.pallas_call(
        functools.partial(_mm_kernel, k_splits=k_splits),
        grid=(N // tn, rows // tm),
        in_specs=in_specs,
        out_specs=pl.BlockSpec((tm, tn), lambda j, i: (i, j)),
        out_shape=jax.ShapeDtypeStruct((rows, N), out_dtype),
        scratch_shapes=[pltpu.VMEM((K, tn), BF16)],
        compiler_params=_params(2),
        name=name,
    )(*a_list, w_all)


def _dot_nt(a, b):
    return lax.dot_general(a, b, (((1,), (1,)), ((), ())), preferred_element_type=F32)


def _rms_gain(o, g):
    ms = jnp.mean(o * o, axis=-1, keepdims=True)
    return (o * lax.rsqrt(ms + RMS_EPS) * g).astype(BF16)


def _swa_mask_table():
    qi = np.arange(SWA_BLOCK)[:, None]
    kj = np.arange(SWA_BLOCK)[None, :]
    prev_ok = kj >= qi
    cur_ok = np.ones((SWA_BLOCK, SWA_BLOCK), bool)
    next_ok = kj <= qi
    none = np.zeros((SWA_BLOCK, SWA_BLOCK), bool)
    kinds = [np.concatenate([none, cur_ok, next_ok], 1),
             np.concatenate([prev_ok, cur_ok, next_ok], 1),
             np.concatenate([prev_ok, cur_ok, none], 1),
             np.concatenate([none, none, none], 1)]
    return jnp.asarray(np.where(np.stack(kinds), 0.0, NEG_INF).astype(np.float32))


def _swa_kernel(sink_ref, q_ref, kp_ref, kc_ref, kn_ref, vp_ref, vc_ref, vn_ref, kx_ref, vx_ref,
                mask_ref, g_ref, o_ref, acc_ref, *, kv_heads):
    n_ctx = kx_ref.shape[0]
    mask = jnp.concatenate([mask_ref[0], jnp.zeros((SWA_BLOCK, n_ctx), F32)], axis=1)
    nk = mask.shape[1]
    for hk in range(kv_heads):
        ks = slice(hk * HEAD_DIM, (hk + 1) * HEAD_DIM)
        q = jnp.concatenate([q_ref[:, (hk * SWA_GROUP + g) * HEAD_DIM:(hk * SWA_GROUP + g + 1) * HEAD_DIM]
                             for g in range(SWA_GROUP)], axis=0)
        kb = jnp.concatenate([kp_ref[:, ks], kc_ref[:, ks], kn_ref[:, ks], kx_ref[:, ks]], axis=0)
        vb = jnp.concatenate([vp_ref[:, ks], vc_ref[:, ks], vn_ref[:, ks], vx_ref[:, ks]], axis=0)
        vb = jnp.concatenate([vb, jnp.ones((nk, HEAD_DIM), BF16)], axis=1)
        s = _dot_nt(q, kb)
        s = (s.reshape(SWA_GROUP, SWA_BLOCK, nk) + mask[None]).reshape(SWA_GROUP * SWA_BLOCK, nk)
        sink = jnp.concatenate([jnp.full((SWA_BLOCK, 1), sink_ref[hk * SWA_GROUP + g], F32)
                                for g in range(SWA_GROUP)], axis=0)
        m = jnp.maximum(jnp.max(s, axis=-1, keepdims=True), sink)
        p = jnp.exp(s - m)
        o = jnp.dot(p.astype(BF16), vb, preferred_element_type=F32)
        denom = o[:, HEAD_DIM:HEAD_DIM + 1] + jnp.exp(sink - m)
        o = o[:, :HEAD_DIM] / denom
        for g in range(SWA_GROUP):
            h = hk * SWA_GROUP + g
            acc_ref[:, h * HEAD_DIM:(h + 1) * HEAD_DIM] = o[g * SWA_BLOCK:(g + 1) * SWA_BLOCK]
    o_ref[...] = _rms_gain(acc_ref[...], g_ref[...])


def _swa(dm, u_rot, u_rest, sink, mask_tab, gain):
    nb = dm.S // SWA_BLOCK
    lat_blocks = dm.B * nb
    ctx_per = dm.L // SWA_BLOCK
    n_blocks = lat_blocks + dm.B * ctx_per
    kvw = dm.swa_kv
    k_col = dm.off_swa_k // kvw
    v_col = dm.rest_swa_v // kvw
    ctx_row0 = dm.n_lat // dm.L

    def batch_of(g):
        return jnp.where(g < lat_blocks, g // nb, (g - lat_blocks) // ctx_per)

    def nbr(g, d):
        n = g % nb
        return jnp.where(g < lat_blocks, (g // nb) * nb + jnp.clip(n + d, 0, nb - 1), g)

    def kind(g):
        n = g % nb
        return jnp.where(g < lat_blocks, jnp.where(n == 0, 0, jnp.where(n == nb - 1, 2, 1)), 3)

    blk = lambda col, d: pl.BlockSpec((SWA_BLOCK, kvw), lambda g: (nbr(g, d), col))
    ctx = lambda col: pl.BlockSpec((dm.L, kvw), lambda g: (ctx_row0 + batch_of(g), col))
    in_specs = [pl.BlockSpec(memory_space=pltpu.SMEM),
                pl.BlockSpec((SWA_BLOCK, dm.swa_q), lambda g: (g, 0)),
                blk(k_col, -1), blk(k_col, 0), blk(k_col, 1),
                blk(v_col, -1), blk(v_col, 0), blk(v_col, 1),
                ctx(k_col), ctx(v_col),
                pl.BlockSpec((1, SWA_BLOCK, 3 * SWA_BLOCK), lambda g: (kind(g), 0, 0)),
                pl.BlockSpec((1, dm.swa_q), lambda g: (0, 0))]
    return pl.pallas_call(
        functools.partial(_swa_kernel, kv_heads=dm.swa_kv_heads),
        grid=(n_blocks,),
        in_specs=in_specs,
        out_specs=pl.BlockSpec((SWA_BLOCK, dm.swa_q), lambda g: (g, 0)),
        out_shape=jax.ShapeDtypeStruct((dm.rows, dm.swa_q), BF16),
        scratch_shapes=[pltpu.VMEM((SWA_BLOCK, dm.swa_q), F32)],
        compiler_params=_params(1),
        name="window_attn",
    )(sink, u_rot, u_rot, u_rot, u_rot, u_rest, u_rest, u_rest, u_rot, u_rest, mask_tab, gain)


def _nat_bias_tables(dm, rpb_all):
    nq, nk = NAT_QROWS, 3 * NAT_QROWS
    i = np.arange(nq)[:, None]
    j = np.arange(nk)[None, :]
    row_ok = [(j >= nq) & (j < nq + NAT_ROWS) & (i >= 0),
              (j - i >= 0) & (j - i < NAT_ROWS),
              (j >= 0) & (j < NAT_ROWS) & (i >= 0),
              np.zeros((nq, nk), bool)]
    dr = np.clip(j - i + (NAT_ROWS - 1) - nq, 0, 2 * NAT_ROWS - 2)
    t_row = np.zeros((2 * NAT_ROWS - 1, nq, nk), np.float32)
    for a in range(nq):
        for b in range(nk):
            t_row[dr[a, b], a, b] = 1.0
    cq = np.arange(GRID_W)[:, None]
    kc = np.arange(GRID_W)[None, :]
    cs = np.clip(cq - NAT_COLS // 2, 0, GRID_W - NAT_COLS)
    col_ok = (kc >= cs) & (kc < cs + NAT_COLS)
    dc = np.clip(kc - cq + NAT_COLS - 1, 0, 2 * NAT_COLS - 2)
    t_col = np.zeros((2 * NAT_COLS - 1, GRID_W, GRID_W), np.float32)
    for a in range(GRID_W):
        for b in range(GRID_W):
            t_col[dc[a, b], a, b] = 1.0
    val = jnp.einsum('lhrc,rij,cqk->lhiqjk', rpb_all, jnp.asarray(t_row), jnp.asarray(t_col),
                     precision=lax.Precision.HIGHEST)
    ok = jnp.asarray(np.broadcast_to(col_ok[None, :, None, :], (nq, GRID_W, nk, GRID_W)))
    bias = jnp.where(ok[None, None], val.astype(BF16), jnp.asarray(NEG_INF, BF16))
    depth, H = bias.shape[:2]
    bias = bias.reshape(depth, H, NAT_QBLOCK, 3 * NAT_QBLOCK)
    kmask = np.stack([np.broadcast_to(r[:, None, :, None], (nq, GRID_W, nk, GRID_W)).reshape(NAT_QBLOCK, 3 * NAT_QBLOCK)
                      for r in row_ok])
    kmask = jnp.asarray(np.where(kmask, 0.0, NEG_INF).astype(np.float32)).astype(BF16)
    return bias, kmask


def _nat_kernel(q_ref, kp_ref, kc_ref, kn_ref, vp_ref, vc_ref, vn_ref, kx_ref, vx_ref,
                bias_ref, kmask_ref, g_ref, o_ref, acc_ref, *, heads):
    n_ctx = kx_ref.shape[0]
    no_bias = jnp.zeros((NAT_QBLOCK, n_ctx), F32)
    kmask = kmask_ref[0].astype(F32)
    nk = 3 * NAT_QBLOCK + n_ctx
    for h in range(heads):
        hs = slice(h * HEAD_DIM, (h + 1) * HEAD_DIM)
        q = q_ref[:, hs]
        kb = jnp.concatenate([kp_ref[:, hs], kc_ref[:, hs], kn_ref[:, hs], kx_ref[:, hs]], axis=0)
        vb = jnp.concatenate([vp_ref[:, hs], vc_ref[:, hs], vn_ref[:, hs], vx_ref[:, hs]], axis=0)
        vb = jnp.concatenate([vb, jnp.ones((nk, HEAD_DIM), BF16)], axis=1)
        s = _dot_nt(q, kb) + jnp.concatenate([bias_ref[h].astype(F32) + kmask, no_bias], axis=1)
        m = jnp.max(s, axis=-1, keepdims=True)
        p = jnp.exp(s - m)
        o = jnp.dot(p.astype(BF16), vb, preferred_element_type=F32)
        acc_ref[:, hs] = o[:, :HEAD_DIM] / o[:, HEAD_DIM:HEAD_DIM + 1]
    o_ref[...] = _rms_gain(acc_ref[...], g_ref[...])


def _nat(dm, u, bias_tab, kmask, gain):
    nb = dm.S // NAT_QBLOCK
    lat_blocks = dm.B * nb
    ctx_per = dm.L // NAT_QBLOCK
    n_blocks = lat_blocks + dm.B * ctx_per
    w = dm.nat_w
    q_col, k_col, v_col = dm.rest_nat_q // w, dm.rest_nat_k // w, dm.rest_nat_v // w
    ctx_row0 = dm.n_lat // dm.L

    def batch_of(g):
        return jnp.where(g < lat_blocks, g // nb, (g - lat_blocks) // ctx_per)

    def nbr(g, d):
        n = g % nb
        return jnp.where(g < lat_blocks, (g // nb) * nb + jnp.clip(n + d, 0, nb - 1), g)

    def kind(g):
        n = g % nb
        return jnp.where(g < lat_blocks, jnp.where(n == 0, 0, jnp.where(n == nb - 1, 2, 1)), 3)

    blk = lambda col, d: pl.BlockSpec((NAT_QBLOCK, w), lambda g: (nbr(g, d), col))
    ctx = lambda col: pl.BlockSpec((dm.L, w), lambda g: (ctx_row0 + batch_of(g), col))
    in_specs = [pl.BlockSpec((NAT_QBLOCK, w), lambda g: (g, q_col)),
                blk(k_col, -1), blk(k_col, 0), blk(k_col, 1),
                blk(v_col, -1), blk(v_col, 0), blk(v_col, 1),
                ctx(k_col), ctx(v_col),
                pl.BlockSpec((dm.nat_heads, NAT_QBLOCK, 3 * NAT_QBLOCK), lambda g: (0, 0, 0)),
                pl.BlockSpec((1, NAT_QBLOCK, 3 * NAT_QBLOCK), lambda g: (kind(g), 0, 0)),
                pl.BlockSpec((1, w), lambda g: (0, 0))]
    return pl.pallas_call(
        functools.partial(_nat_kernel, heads=dm.nat_heads),
        grid=(n_blocks,),
        in_specs=in_specs,
        out_specs=pl.BlockSpec((NAT_QBLOCK, w), lambda g: (g, 0)),
        out_shape=jax.ShapeDtypeStruct((dm.rows, w), BF16),
        scratch_shapes=[pltpu.VMEM((NAT_QBLOCK, w), F32)],
        compiler_params=_params(1),
        name="nbr_attn",
    )(u, u, u, u, u, u, u, u, u, bias_tab, kmask, gain)


CONV_TILE = 256
HALO = 16


def _conv_kernel(u_ref, prev_ref, next_ref, w_ref, b_ref, g_ref, o_ref, *, cw, tiles_per_seq, lat_tiles):
    i = pl.program_id(0)
    is_ctx = i >= lat_tiles
    n = i % tiles_per_seq
    has_prev = jnp.logical_and(jnp.logical_not(is_ctx), n != 0)
    has_next = jnp.logical_and(jnp.logical_not(is_ctx), n != tiles_per_seq - 1)
    x = u_ref[:, 0:cw].astype(F32)
    bg = u_ref[:, cw:2 * cw].astype(F32)
    cg = u_ref[:, 2 * cw:3 * cw].astype(F32)
    z = cg * x
    zp = (prev_ref[HALO - 1:HALO, 2 * cw:3 * cw].astype(F32) * prev_ref[HALO - 1:HALO, 0:cw].astype(F32))
    zn = (next_ref[0:1, 2 * cw:3 * cw].astype(F32) * next_ref[0:1, 0:cw].astype(F32))
    zp = jnp.where(has_prev, zp, 0.0)
    zn = jnp.where(has_next, zn, 0.0)
    row = lax.broadcasted_iota(jnp.int32, z.shape, 0)
    z_m1 = jnp.where(row == 0, zp, pltpu.roll(z, 1, 0))
    z_p1 = jnp.where(row == CONV_TILE - 1, zn, pltpu.roll(z, CONV_TILE - 1, 0))
    w = w_ref[...]
    conv = b_ref[...] + z_m1 * w[0:1] + z * w[1:2] + z_p1 * w[2:3]
    o_ref[...] = _rms_gain(bg * conv, g_ref[...])


def _conv(dm, u, w, b, gain):
    assert dm.L == CONV_TILE, "context sequences are one convolution tile"
    cw = dm.conv_w
    tiles = dm.rows // CONV_TILE
    lat_tiles = dm.n_lat // CONV_TILE
    col = 0
    per_halo = CONV_TILE // HALO
    last_halo = dm.rows // HALO - 1
    kern = functools.partial(_conv_kernel, cw=cw, tiles_per_seq=dm.S // CONV_TILE, lat_tiles=lat_tiles)
    return pl.pallas_call(
        kern,
        grid=(tiles,),
        in_specs=[pl.BlockSpec((CONV_TILE, 3 * cw), lambda i: (i, col)),
                  pl.BlockSpec((HALO, 3 * cw), lambda i: (jnp.maximum(i * per_halo - 1, 0), col)),
                  pl.BlockSpec((HALO, 3 * cw), lambda i: (jnp.minimum((i + 1) * per_halo, last_halo), col)),
                  pl.BlockSpec((3, cw), lambda i: (0, 0)),
                  pl.BlockSpec((1, cw), lambda i: (0, 0)),
                  pl.BlockSpec((1, cw), lambda i: (0, 0))],
        out_specs=pl.BlockSpec((CONV_TILE, cw), lambda i: (i, 0)),
        out_shape=jax.ShapeDtypeStruct((dm.rows, cw), BF16),
        compiler_params=_params(1),
        name="gated_conv",
    )(u, u, u, w, b, gain)


def _layer_norm(y, g, b):
    mu = jnp.mean(y, axis=-1, keepdims=True)
    yc = y - mu
    var = jnp.mean(yc * yc, axis=-1, keepdims=True)
    return yc * lax.rsqrt(var + LN_EPS) * g + b


def _route(logits):
    lane = lax.broadcasted_iota(jnp.int32, logits.shape, 1).astype(F32)
    big = float(ROUTER_LANES)
    is_g = lane < N_GROUPS
    gl = jnp.where(is_g, logits, NEG_INF)
    gmax = jnp.max(gl, axis=-1, keepdims=True)
    ge = jnp.where(is_g, jnp.exp(gl - gmax), 0.0)
    gprob = ge / jnp.sum(ge, axis=-1, keepdims=True)
    g_p = jnp.max(gprob, axis=-1, keepdims=True)
    g_idx = jnp.min(jnp.where(jnp.logical_and(is_g, gprob == g_p), lane, big), axis=-1, keepdims=True)
    lo = N_GROUPS + g_idx * EXPERTS_PER_GROUP
    in_grp = jnp.logical_and(lane >= lo, lane < lo + EXPERTS_PER_GROUP)
    el = jnp.where(in_grp, logits, NEG_INF)
    v1 = jnp.max(el, axis=-1, keepdims=True)
    i1 = jnp.min(jnp.where(jnp.logical_and(in_grp, el == v1), lane, big), axis=-1, keepdims=True)
    el2 = jnp.where(lane == i1, NEG_INF, el)
    v2 = jnp.max(el2, axis=-1, keepdims=True)
    rest = jnp.logical_and(in_grp, lane != i1)
    i2 = jnp.min(jnp.where(jnp.logical_and(rest, el2 == v2), lane, big), axis=-1, keepdims=True)
    e2 = jnp.exp(v2 - v1)
    den = 1.0 + e2
    w1 = g_p / den
    w2 = g_p * (e2 / den)
    gates = jnp.where(lane == i1, w1, jnp.where(lane == i2, w2, 0.0))
    return jnp.where(lane == ROUTE_GROUP_LANE, g_idx, gates)


HIGH_HALF = 0xFFFF0000


def _pack_bf16_pairs(x):
    w = x.shape[1] // 2
    hi = lax.bitcast_convert_type(x[:, :w].astype(BF16).astype(F32), jnp.uint32)
    lo = lax.bitcast_convert_type(x[:, w:].astype(BF16).astype(F32), jnp.uint32)
    return hi | (lo >> 16)


def _unpack_bf16_pairs(p):
    hi = lax.bitcast_convert_type(p & jnp.uint32(HIGH_HALF), F32)
    lo = lax.bitcast_convert_type(p << 16, F32)
    return hi, lo


def _start_row_gather(idx_ref, idx0, src_hbm, dst_ref, sem, row0, n_rows):
    def issue(k, carry):
        j = row0 + k
        row = idx_ref[idx0 + j]
        pltpu.make_async_copy(src_hbm.at[pl.ds(row, 1)], dst_ref.at[pl.ds(j, 1)], sem).start()
        return carry

    lax.fori_loop(0, n_rows, issue, 0, unroll=8)


def _wait_row_gather(src_hbm, dst_ref, sem):
    pltpu.make_async_copy(src_hbm.at[pl.ds(0, dst_ref.shape[0])], dst_ref, sem).wait()


def _ln1_kernel(h_ref, mix_ref, mod_ref, g_ref, b_ref, wr_ref, wl_ref, br_ref, h_out, t_out, route_out, *,
                D, alpha):
    mod = mod_ref[0]
    y = alpha * h_ref[...] + mod[:, 2 * D:3 * D] * mix_ref[...].astype(F32)
    hn = _layer_norm(y, g_ref[...], b_ref[...])
    h_out[...] = hn
    t = hn * (1.0 + mod[:, 4 * D:5 * D]) + mod[:, 3 * D:4 * D]
    t_hi = t.astype(BF16)
    t_lo = (t - t_hi.astype(F32)).astype(BF16)
    logits = (jnp.dot(t_hi, wr_ref[...], preferred_element_type=F32)
              + jnp.dot(t_lo, wr_ref[...], preferred_element_type=F32)
              + jnp.dot(t_hi, wl_ref[...], preferred_element_type=F32)) + br_ref[...]
    route = _route(logits)
    route_out[...] = route
    t_out[:, 0:D // 2] = _pack_bf16_pairs(t)
    t_out[:, D // 2:] = lax.bitcast_convert_type(route, jnp.uint32)


def _ln1(dm, rows, h, mix, mod, g, b, wr_hi, wr_lo, br):
    tr, D = dm.tr, dm.D
    row = lambda width: pl.BlockSpec((tr, width), lambda i: (i, 0))
    vec = lambda width: pl.BlockSpec((1, width), lambda i: (0, 0))
    return pl.pallas_call(
        functools.partial(_ln1_kernel, D=D, alpha=dm.alpha),
        grid=(rows // tr,),
        in_specs=[row(D), row(D), _mod_spec(dm, tr), vec(D), vec(D),
                  pl.BlockSpec((D, ROUTER_LANES), lambda i: (0, 0)),
                  pl.BlockSpec((D, ROUTER_LANES), lambda i: (0, 0)), vec(ROUTER_LANES)],
        out_specs=[row(D), row(dm.packed_w), row(ROUTER_LANES)],
        out_shape=[jax.ShapeDtypeStruct((rows, D), F32), jax.ShapeDtypeStruct((rows, dm.packed_w), jnp.uint32),
                   jax.ShapeDtypeStruct((rows, ROUTER_LANES), F32)],
        compiler_params=_params(1),
        name="mix_residual_norm_route",
    )(h, mix, mod, g, b, wr_hi, wr_lo, br)


def _ln2_kernel(pos_ref, h_ref, f_hbm, mod_ref, modn_ref, g_ref, b_ref, h_out, a_out, fbuf, sem, *,
                D, alpha, tr, n_tiles):
    i = pl.program_id(0)
    slot = i % 2

    @pl.when(i == 0)
    def _():
        _start_row_gather(pos_ref, 0, f_hbm, fbuf.at[0], sem.at[0], 0, tr)

    @pl.when(i + 1 < n_tiles)
    def _():
        _start_row_gather(pos_ref, (i + 1) * tr, f_hbm, fbuf.at[1 - slot], sem.at[1 - slot], 0, tr)

    _wait_row_gather(f_hbm, fbuf.at[slot], sem.at[slot])
    mod = mod_ref[0]
    f_hi, f_lo = _unpack_bf16_pairs(fbuf[slot])
    ffn = jnp.concatenate([f_hi, f_lo], axis=1)
    y = alpha * h_ref[...] + mod[:, 5 * D:6 * D] * ffn
    hn = _layer_norm(y, g_ref[...], b_ref[...])
    h_out[...] = hn
    if a_out is not None:
        modn = modn_ref[0]
        a_out[...] = (hn * (1.0 + modn[:, D:2 * D]) + modn[:, 0:D]).astype(BF16)


def _ln2_last_kernel(pos_ref, h_ref, f_hbm, mod_ref, g_ref, b_ref, h_out, fbuf, sem, **kw):
    _ln2_kernel(pos_ref, h_ref, f_hbm, mod_ref, None, g_ref, b_ref, h_out, None, fbuf, sem, **kw)


def _ln2(dm, rows, pos, h, ffn_sorted, mod, mod_next, g, b):
    tr, D = dm.tr, dm.D
    n_tiles = rows // tr
    row = pl.BlockSpec((tr, D), lambda i, p: (i, 0))
    vec = pl.BlockSpec((1, D), lambda i, p: (0, 0))
    hbm = pl.BlockSpec(memory_space=pl.ANY)
    scratch = [pltpu.VMEM((2, tr, D // 2), jnp.uint32), pltpu.SemaphoreType.DMA((2,))]
    kw = dict(D=D, alpha=dm.alpha, tr=tr, n_tiles=n_tiles)
    if mod_next is None:
        return pl.pallas_call(
            functools.partial(_ln2_last_kernel, **kw),
            grid_spec=pltpu.PrefetchScalarGridSpec(
                num_scalar_prefetch=1, grid=(n_tiles,),
                in_specs=[row, hbm, _mod_spec(dm, tr), vec, vec],
                out_specs=row, scratch_shapes=scratch),
            out_shape=jax.ShapeDtypeStruct((rows, D), F32),
            compiler_params=_params(1),
            name="ffn_residual_norm_last",
        )(pos, h, ffn_sorted, mod, g, b), None
    return pl.pallas_call(
        functools.partial(_ln2_kernel, **kw),
        grid_spec=pltpu.PrefetchScalarGridSpec(
            num_scalar_prefetch=1, grid=(n_tiles,),
            in_specs=[row, hbm, _mod_spec(dm, tr), _mod_spec(dm, tr), vec, vec],
            out_specs=[row, row], scratch_shapes=scratch),
        out_shape=[jax.ShapeDtypeStruct((rows, D), F32), jax.ShapeDtypeStruct((rows, D), BF16)],
        compiler_params=_params(1),
        name="ffn_residual_norm",
    )(pos, h, ffn_sorted, mod, mod_next, g, b)


def _dispatch_plan(rows, route):
    tm = EXPERT_TILE
    n_tiles = rows // tm + N_GROUPS
    i32 = jnp.int32
    gidx = route[:, ROUTE_GROUP_LANE].astype(i32)
    onehot = (gidx[:, None] == jnp.arange(N_GROUPS, dtype=i32)[None]).astype(i32)
    csum = jnp.cumsum(onehot, axis=0)
    counts = csum[-1]
    rank = jnp.sum(onehot * csum, axis=1) - 1
    padded = (counts + tm - 1) // tm * tm
    pstart = jnp.cumsum(padded) - padded
    ustart = jnp.cumsum(counts) - counts
    total = jnp.sum(padded)
    pos = jnp.sum(onehot * pstart[None], axis=1) + rank
    order = jnp.argsort(gidx, stable=True).astype(i32)
    slot = jnp.arange(n_tiles * tm, dtype=i32)
    sgrp = jnp.sum((slot[:, None] >= pstart[None, 1:]).astype(i32), axis=1)
    shot = (sgrp[:, None] == jnp.arange(N_GROUPS, dtype=i32)[None]).astype(i32)
    k = slot - jnp.sum(shot * pstart[None], axis=1)
    valid = jnp.logical_and(k < jnp.sum(shot * counts[None], axis=1), slot < total)
    sorted_at = jnp.clip(jnp.sum(shot * ustart[None], axis=1) + k, 0, rows - 1)
    src = jnp.where(valid, order[sorted_at], slot % rows)
    tile_slot = jnp.arange(n_tiles, dtype=i32) * tm
    tgrp = jnp.where(tile_slot < total, sgrp[::tm], N_GROUPS)
    return src, pos.astype(i32), tgrp.astype(i32)


def _experts_kernel(src_ref, tgrp_ref, t_hbm, wg_ref, wu_ref, wd_ref, o_ref, tbuf, sem, tb_ref, gate_ref, acc_ref,
                    *, D, tm, n_tiles):
    i = pl.program_id(0)
    e = pl.program_id(1)
    slot = i % 2
    grp = tgrp_ref[i]
    active = grp < N_GROUPS
    half = D // 2
    part_rows = tm // EXPERT_STEPS
    F = wg_ref.shape[-1]

    @pl.when(jnp.logical_and(i == 0, e == 0))
    def _():
        _start_row_gather(src_ref, 0, t_hbm, tbuf.at[0], sem.at[0], 0, tm)

    @pl.when(e == 0)
    def _():
        _wait_row_gather(t_hbm, tbuf.at[slot], sem.at[slot])
        hi, lo = _unpack_bf16_pairs(tbuf[slot, :, 0:half])
        tb_ref[:, 0:half] = hi.astype(BF16)
        tb_ref[:, half:] = lo.astype(BF16)
        gate_ref[...] = lax.bitcast_convert_type(tbuf[slot, :, half:], F32)

    has_next = i + 1 < n_tiles

    @pl.when(jnp.logical_and(has_next, jnp.logical_not(active)))
    def _():
        _start_row_gather(src_ref, (i + 1) * tm, t_hbm, tbuf.at[1 - slot], sem.at[1 - slot],
                          e * part_rows, part_rows)

    @pl.when(active)
    def _():
        nxt = jnp.minimum(i + 1, n_tiles - 1) * tm + e * part_rows
        for k in range(part_rows):
            pltpu.make_async_copy(t_hbm.at[pl.ds(src_ref[nxt + k], 1)],
                                  tbuf.at[1 - slot, pl.ds(e * part_rows + k, 1)], sem.at[1 - slot]).start()
        t = tb_ref[...]
        gates = gate_ref[...]
        lane = lax.broadcasted_iota(jnp.int32, gates.shape, 1)
        hids = []
        for k in range(EXPERTS_PER_STEP):
            a = jnp.dot(t, wg_ref[k], preferred_element_type=F32)
            b = jnp.dot(t, wu_ref[k], preferred_element_type=F32)
            expert_lane = N_GROUPS + grp * EXPERTS_PER_GROUP + e * EXPERTS_PER_STEP + k
            gcol = jnp.sum(jnp.where(lane == expert_lane, gates, 0.0), axis=-1, keepdims=True)
            hids.append((a * jax.nn.sigmoid(a) * b * gcol).astype(BF16))
        hid = jnp.concatenate(hids, axis=1)
        part = jnp.dot(hid, wd_ref[...].reshape(EXPERTS_PER_STEP * F, D), preferred_element_type=F32)

        @pl.when(e == 0)
        def _():
            acc_ref[...] = part

        @pl.when(e == EXPERT_STEPS - 1)
        def _():
            o_ref[...] = _pack_bf16_pairs(acc_ref[...] + part)

    @pl.when(jnp.logical_and(e == EXPERT_STEPS - 1, jnp.logical_not(active)))
    def _():
        o_ref[...] = jnp.zeros(o_ref.shape, o_ref.dtype)

    @pl.when(jnp.logical_and(active, jnp.logical_and(jnp.logical_not(has_next), e == EXPERT_STEPS - 1)))
    def _():
        _wait_row_gather(t_hbm, tbuf.at[1 - slot], sem.at[1 - slot])


def _experts(dm, layer, src, tgrp, t_packed, wg_all, wu_all, wd_all):
    tm, D, F = EXPERT_TILE, dm.D, dm.d_expert
    n_tiles = tgrp.shape[0]

    def w_map(i, e, sr, tg):
        return (layer, jnp.minimum(tg[i], N_GROUPS - 1) * EXPERT_STEPS + e, 0, 0)

    return pl.pallas_call(
        functools.partial(_experts_kernel, D=D, tm=tm, n_tiles=n_tiles),
        grid_spec=pltpu.PrefetchScalarGridSpec(
            num_scalar_prefetch=2,
            grid=(n_tiles, EXPERT_STEPS),
            in_specs=[pl.BlockSpec(memory_space=pl.ANY),
                      pl.BlockSpec((None, EXPERTS_PER_STEP, D, F), w_map),
                      pl.BlockSpec((None, EXPERTS_PER_STEP, D, F), w_map),
                      pl.BlockSpec((None, EXPERTS_PER_STEP, F, D), w_map)],
            out_specs=pl.BlockSpec((tm, D // 2), lambda i, e, sr, tg: (i, 0)),
            scratch_shapes=[pltpu.VMEM((2, tm, dm.packed_w), jnp.uint32), pltpu.SemaphoreType.DMA((2,)),
                            pltpu.VMEM((tm, D), BF16), pltpu.VMEM((tm, ROUTER_LANES), F32),
                            pltpu.VMEM((tm, D), F32)]),
        out_shape=jax.ShapeDtypeStruct((n_tiles * tm, D // 2), jnp.uint32),
        compiler_params=pltpu.CompilerParams(dimension_semantics=("arbitrary", "arbitrary"),
                                             vmem_limit_bytes=EXPERTS_VMEM_LIMIT),
        name="experts",
    )(src, tgrp, t_packed, wg_all, wu_all, wd_all)


def kernel(x, c, ctx, c_ctx, w_ada, b_ada, w_in, conv_w, conv_b, attn_sink, nat_rpb, mix_norm_g, w_out,
           ln1_g, ln1_b, w_router_group, b_router_group, w_router_expert, b_router_expert,
           w_gate, w_up, w_down, ln2_g, ln2_b):
    B, S, D = x.shape
    L = ctx.shape[1]
    depth = w_in.shape[0]
    dm = Dims(B, S, L, D, depth, w_gate.shape[-1])

    xin = jnp.concatenate([c, c_ctx[None], jnp.zeros((MOD_ROWS - B - 1, D), F32)], axis=0)
    mods = _ada_all(xin, w_ada, b_ada).reshape(depth, MOD_ROWS, 1, 6 * D)

    cos_t, sin_t = _rope_tables(dm)
    swa_mask = _swa_mask_table()
    nat_bias, nat_kmask = _nat_bias_tables(dm, nat_rpb)

    w_gate_b = w_gate.astype(BF16)
    w_up_b = w_up.astype(BF16)
    w_down_b = w_down.astype(BF16)
    pad = jnp.zeros((depth, D, ROUTER_LANES - N_GROUPS - N_EXPERTS), F32)
    w_route = jnp.concatenate([w_router_group, w_router_expert, pad], axis=-1)
    w_route_hi = w_route.astype(BF16)
    w_route_lo = (w_route - w_route_hi.astype(F32)).astype(BF16)
    b_route = jnp.concatenate([b_router_group, b_router_expert, pad[:, 0]], axis=-1).reshape(depth, 1, ROUTER_LANES)

    h, a = _modulate(dm, x.reshape(B * S, D), ctx.reshape(B * L, D), mods[0])
    for i in range(depth):
        last = i == depth - 1
        rows = dm.n_lat if last else dm.rows
        u_rot, u_rest = _inproj(dm, a, w_in, i, cos_t, sin_t)
        gain = mix_norm_g[i].reshape(1, D)
        y_swa = _swa(dm, u_rot, u_rest, attn_sink[i], swa_mask, gain[:, :dm.swa_q])
        y_conv = _conv(dm, u_rest, conv_w[i], conv_b[i].reshape(1, -1), gain[:, dm.swa_q:dm.swa_q + dm.conv_w])
        y_nat = _nat(dm, u_rest, nat_bias[i], nat_kmask, gain[:, dm.swa_q + dm.conv_w:])
        mix = _matmul([y_swa, y_conv, y_nat], w_out, i, rows, dm.tm, 512, BF16, "out_proj")
        h_mid, t_packed, route = _ln1(dm, rows, h, mix, mods[i], ln1_g[i].reshape(1, D), ln1_b[i].reshape(1, D),
                                      w_route_hi[i], w_route_lo[i], b_route[i])
        src, pos, tgrp = _dispatch_plan(rows, route)
        ffn_sorted = _experts(dm, i, src, tgrp, t_packed, w_gate_b, w_up_b, w_down_b)
        h, a = _ln2(dm, rows, pos, h_mid, ffn_sorted, mods[i], None if last else mods[i + 1],
                    ln2_g[i].reshape(1, D), ln2_b[i].reshape(1, D))
    return h.reshape(B, S, D)
```

```python
import functools

import numpy as np
import jax
import jax.numpy as jnp
from jax import lax
from jax.experimental import pallas as pl
from jax.experimental.pallas import tpu as pltpu

F32 = jnp.float32
BF16 = jnp.bfloat16

HEAD_DIM = 128
LANES = 128
GRID_W = 64
ROPE_THETA = 10000.0
ROPE_FREQS = HEAD_DIM // 4
SWA_GROUP = 4
SWA_BLOCK = 128
NAT_ROWS = 8
NAT_COLS = 16
NAT_QROWS = 4
NAT_QBLOCK = NAT_QROWS * GRID_W
N_GROUPS = 4
EXPERTS_PER_GROUP = 4
N_EXPERTS = N_GROUPS * EXPERTS_PER_GROUP
ROUTER_LANES = 128
ROUTE_GROUP_LANE = 0
EXPERT_TILE = 512
EXPERTS_PER_STEP = 2
EXPERT_STEPS = EXPERTS_PER_GROUP // EXPERTS_PER_STEP
EXPERTS_VMEM_LIMIT = 62 * 1024 * 1024
LN_EPS = 1e-5
RMS_EPS = 1e-6
NEG_INF = -1e30
ATTN_SCALE = HEAD_DIM ** -0.5
MOD_ROWS = 8
VMEM_LIMIT = 56 * 1024 * 1024


def _params(n_axes):
    return pltpu.CompilerParams(dimension_semantics=("arbitrary",) * n_axes,
                                vmem_limit_bytes=VMEM_LIMIT)


class Dims:
    def __init__(self, B, S, L, D, depth, d_expert):
        self.B, self.S, self.L, self.D, self.depth, self.d_expert = B, S, L, D, depth, d_expert
        self.swa_q = D // 2
        self.swa_heads = self.swa_q // HEAD_DIM
        self.swa_kv_heads = self.swa_heads // SWA_GROUP
        self.swa_kv = self.swa_kv_heads * HEAD_DIM
        self.conv_w = D // 4
        self.nat_w = D // 4
        self.nat_heads = self.nat_w // HEAD_DIM
        self.off_swa_k = self.swa_q
        self.off_swa_v = self.off_swa_k + self.swa_kv
        self.off_conv = self.off_swa_v + self.swa_kv
        self.off_nat_q = self.off_conv + 3 * self.conv_w
        self.off_nat_k = self.off_nat_q + self.nat_w
        self.off_nat_v = self.off_nat_k + self.nat_w
        self.in_cols = self.off_nat_v + self.nat_w
        self.rest_nat_q = 3 * self.conv_w
        self.rest_nat_k = self.rest_nat_q + self.nat_w
        self.rest_nat_v = self.rest_nat_k + self.nat_w
        self.rest_swa_v = self.rest_nat_v + self.nat_w
        self.n_lat = B * S
        self.n_ctx = B * L
        self.rows = self.n_lat + self.n_ctx
        self.grid_rows = S // GRID_W
        self.packed_w = D // 2 + ROUTER_LANES
        assert self.rows % EXPERT_TILE == 0 and self.n_lat % EXPERT_TILE == 0
        self.alpha = (2.0 * depth) ** 0.25
        self.tm = 1024 if (S % 1024 == 0 and self.n_ctx % 1024 == 0) else 256
        self.tr = 256
        self.tn = self.swa_kv
        assert S % self.tm == 0 and self.n_ctx % self.tm == 0
        assert S % NAT_QBLOCK == 0 and L % NAT_QBLOCK == 0 and self.grid_rows >= 3 * NAT_QROWS
        assert S // SWA_BLOCK >= 3 and L % SWA_BLOCK == 0
        assert self.off_conv % (3 * self.conv_w) == 0
        assert B + 1 <= MOD_ROWS


def _ada_kernel(x_ref, w_ref, b_ref, o_ref):
    x = x_ref[...]
    act = x * jax.nn.sigmoid(x)
    a_hi = act.astype(BF16)
    a_lo = (act - a_hi.astype(F32)).astype(BF16)
    w = w_ref[0]
    w_hi = w.astype(BF16)
    w_lo = (w - w_hi.astype(F32)).astype(BF16)
    acc = jnp.dot(a_hi, w_hi, preferred_element_type=F32)
    acc += jnp.dot(a_lo, w_hi, preferred_element_type=F32)
    acc += jnp.dot(a_hi, w_lo, preferred_element_type=F32)
    o_ref[0] = acc + b_ref[0]


def _ada_all(xin, w_ada, b_ada):
    depth, D, n6 = w_ada.shape
    tn = 512
    return pl.pallas_call(
        _ada_kernel,
        grid=(depth, n6 // tn),
        in_specs=[pl.BlockSpec((MOD_ROWS, D), lambda l, j: (0, 0)),
                  pl.BlockSpec((1, D, tn), lambda l, j: (l, 0, j)),
                  pl.BlockSpec((1, 1, tn), lambda l, j: (l, 0, j))],
        out_specs=pl.BlockSpec((1, MOD_ROWS, tn), lambda l, j: (l, 0, j)),
        out_shape=jax.ShapeDtypeStruct((depth, MOD_ROWS, n6), F32),
        compiler_params=_params(2),
        name="ada_mod",
    )(xin, w_ada, b_ada.reshape(depth, 1, n6))


def _mod_spec(dm, tile):
    n6 = 6 * dm.D
    return pl.BlockSpec((1, 1, n6), lambda i, *_: (jnp.minimum(i * tile // dm.S, dm.B), 0, 0))


def _modulate_kernel(x_ref, c_ref, mod_ref, h_out, a_out, *, D, lat_tiles):
    mod = mod_ref[0]

    def emit(v):
        h_out[...] = v
        a_out[...] = (v * (1.0 + mod[:, D:2 * D]) + mod[:, 0:D]).astype(BF16)

    @pl.when(pl.program_id(0) < lat_tiles)
    def _():
        emit(x_ref[...])

    @pl.when(pl.program_id(0) >= lat_tiles)
    def _():
        emit(c_ref[...])


def _modulate(dm, x_rows, ctx_rows, mod):
    tr = dm.tr
    lat_tiles = dm.n_lat // tr
    row = pl.BlockSpec((tr, dm.D), lambda i: (i, 0))
    return pl.pallas_call(
        functools.partial(_modulate_kernel, D=dm.D, lat_tiles=lat_tiles),
        grid=(dm.rows // tr,),
        in_specs=[pl.BlockSpec((tr, dm.D), lambda i: (jnp.minimum(i, lat_tiles - 1), 0)),
                  pl.BlockSpec((tr, dm.D), lambda i: (jnp.maximum(i - lat_tiles, 0), 0)),
                  _mod_spec(dm, tr)],
        out_specs=[row, row],
        out_shape=[jax.ShapeDtypeStruct((dm.rows, dm.D), F32), jax.ShapeDtypeStruct((dm.rows, dm.D), BF16)],
        compiler_params=_params(1),
        name="modulate_in",
    )(x_rows, ctx_rows, mod)


def _rope_tables(dm):
    t = np.arange(dm.S)
    pos = np.stack([t // GRID_W, t % GRID_W], axis=-1).astype(np.float32)
    inv_freq = jnp.asarray(ROPE_THETA, F32) ** (-jnp.arange(ROPE_FREQS, dtype=F32) / ROPE_FREQS)
    ang = jnp.asarray(pos)[:, :, None] * inv_freq
    cos, sin = jnp.cos(ang), jnp.sin(ang)
    cos_t = jnp.concatenate([cos, cos], axis=-1).reshape(dm.S, HEAD_DIM)
    sin_t = jnp.concatenate([-sin, sin], axis=-1).reshape(dm.S, HEAD_DIM)
    cos_t = jnp.concatenate([cos_t, jnp.ones((dm.tm, HEAD_DIM), F32)], axis=0)
    sin_t = jnp.concatenate([sin_t, jnp.zeros((dm.tm, HEAD_DIM), F32)], axis=0)
    return cos_t, sin_t


def _resident_weight_dot(a_ref, w_ref, wb_ref):
    @pl.when(pl.program_id(1) == 0)
    def _():
        wb_ref[...] = w_ref[...].astype(BF16)

    return jnp.dot(a_ref[...], wb_ref[...], preferred_element_type=F32)


def _inproj_rope_kernel(a_ref, w_ref, cos_ref, sin_ref, o_ref, wb_ref, *, tn, n_q):
    acc = _resident_weight_dot(a_ref, w_ref, wb_ref)
    cos = cos_ref[...]
    sin = sin_ref[...]
    scale = jnp.where(pl.program_id(0) < n_q, ATTN_SCALE, 1.0).astype(F32)
    lane = lax.broadcasted_iota(jnp.int32, cos.shape, 1)
    first_half = (lane % (2 * ROPE_FREQS)) < ROPE_FREQS
    for c in range(tn // HEAD_DIM):
        x = acc[:, c * HEAD_DIM:(c + 1) * HEAD_DIM]
        partner = jnp.where(first_half,
                            pltpu.roll(x, HEAD_DIM - ROPE_FREQS, 1),
                            pltpu.roll(x, ROPE_FREQS, 1))
        o_ref[:, c * HEAD_DIM:(c + 1) * HEAD_DIM] = ((x * cos + partner * sin) * scale).astype(BF16)


def _inproj_plain_kernel(a_ref, w_ref, o_ref, wb_ref, *, natq0, natq1):
    j = pl.program_id(0)
    acc = _resident_weight_dot(a_ref, w_ref, wb_ref)
    o_ref[...] = acc.astype(BF16)

    @pl.when(jnp.logical_and(j >= natq0, j < natq1))
    def _():
        o_ref[...] = (acc * ATTN_SCALE).astype(BF16)


def _inproj(dm, a, w_all, layer, cos_t, sin_t):
    tm, tn = dm.tm, dm.tn
    lat_tiles = dm.n_lat // tm
    per_seq = dm.S // tm
    row_tiles = dm.rows // tm

    def tab_map(j, i):
        return (jnp.where(i < lat_tiles, i % per_seq, per_seq), 0)

    a_spec = pl.BlockSpec((tm, dm.D), lambda j, i: (i, 0))
    o_spec = pl.BlockSpec((tm, tn), lambda j, i: (i, j))
    scratch = [pltpu.VMEM((dm.D, tn), BF16)]
    n_rot = dm.off_swa_v // tn
    u_rot = pl.pallas_call(
        functools.partial(_inproj_rope_kernel, tn=tn, n_q=dm.swa_q // tn),
        grid=(n_rot, row_tiles),
        in_specs=[a_spec,
                  pl.BlockSpec((None, dm.D, tn), lambda j, i: (layer, 0, j)),
                  pl.BlockSpec((tm, HEAD_DIM), tab_map),
                  pl.BlockSpec((tm, HEAD_DIM), tab_map)],
        out_specs=o_spec,
        out_shape=jax.ShapeDtypeStruct((dm.rows, dm.off_swa_v), BF16),
        scratch_shapes=scratch,
        compiler_params=_params(2),
        name="in_proj_rotary",
    )(a, w_all, cos_t, sin_t)

    n_rest = (dm.in_cols - dm.off_swa_v) // tn
    conv_tile0 = dm.off_conv // tn
    swa_v_tile = dm.off_swa_v // tn

    def w_map(j, i):
        return (layer, 0, jnp.where(j < n_rest - 1, j + conv_tile0, swa_v_tile))

    u_rest = pl.pallas_call(
        functools.partial(_inproj_plain_kernel, natq0=dm.rest_nat_q // tn, natq1=dm.rest_nat_k // tn),
        grid=(n_rest, row_tiles),
        in_specs=[a_spec, pl.BlockSpec((None, dm.D, tn), w_map)],
        out_specs=o_spec,
        out_shape=jax.ShapeDtypeStruct((dm.rows, dm.in_cols - dm.off_swa_v), BF16),
        scratch_shapes=scratch,
        compiler_params=_params(2),
        name="in_proj_rest",
    )(a, w_all)
    return u_rot, u_rest


def _mm_kernel(*refs, k_splits):
    a_refs, w_ref, o_ref, wb_ref = refs[:-3], refs[-3], refs[-2], refs[-1]

    @pl.when(pl.program_id(1) == 0)
    def _():
        wb_ref[...] = w_ref[...].astype(BF16)

    acc = None
    k0 = 0
    for a_ref, kw in zip(a_refs, k_splits):
        part = jnp.dot(a_ref[...], wb_ref[k0:k0 + kw, :], preferred_element_type=F32)
        acc = part if acc is None else acc + part
        k0 += kw
    o_ref[...] = acc.astype(o_ref.dtype)


def _matmul(a_list, w_all, layer, rows, tm, tn, out_dtype, name):
    _, K, N = w_all.shape
    k_splits = tuple(a.shape[1] for a in a_list)
    assert sum(k_splits) == K
    in_specs = [pl.BlockSpec((tm, kw), lambda j, i: (i, 0)) for kw in k_splits]
    in_specs.append(pl.BlockSpec((None, K, tn), lambda j, i: (layer, 0, j)))
    return pl.pallas_call(
        functools.partial(_mm_kernel, k_splits=k_splits),
        grid=(N // tn, rows // tm),
        in_specs=in_specs,
        out_specs=pl.BlockSpec((tm, tn), lambda j, i: (i, j)),
        out_shape=jax.ShapeDtypeStruct((rows, N), out_dtype),
        scratch_shapes=[pltpu.VMEM((K, tn), BF16)],
        compiler_params=_params(2),
        name=name,
    )(*a_list, w_all)


def _dot_nt(a, b):
    return lax.dot_general(a, b, (((1,), (1,)), ((), ())), preferred_element_type=F32)


def _rms_gain(o, g):
    ms = jnp.mean(o * o, axis=-1, keepdims=True)
    return (o * lax.rsqrt(ms + RMS_EPS) * g).astype(BF16)


def _swa_mask_table():
    qi = np.arange(SWA_BLOCK)[:, None]
    kj = np.arange(SWA_BLOCK)[None, :]
    prev_ok = kj >= qi
    cur_ok = np.ones((SWA_BLOCK, SWA_BLOCK), bool)
    next_ok = kj <= qi
    none = np.zeros((SWA_BLOCK, SWA_BLOCK), bool)
    kinds = [np.concatenate([none, cur_ok, next_ok], 1),
             np.concatenate([prev_ok, cur_ok, next_ok], 1),
             np.concatenate([prev_ok, cur_ok, none], 1),
             np.concatenate([none, none, none], 1)]
    return jnp.asarray(np.where(np.stack(kinds), 0.0, NEG_INF).astype(np.float32))


def _swa_kernel(sink_ref, q_ref, kp_ref, kc_ref, kn_ref, vp_ref, vc_ref, vn_ref, kx_ref, vx_ref,
                mask_ref, g_ref, o_ref, acc_ref, *, kv_heads):
    n_ctx = kx_ref.shape[0]
    mask = jnp.concatenate([mask_ref[0], jnp.zeros((SWA_BLOCK, n_ctx), F32)], axis=1)
    nk = mask.shape[1]
    scores, sinks = [], []
    for hk in range(kv_heads):
        ks = slice(hk * HEAD_DIM, (hk + 1) * HEAD_DIM)
        q = jnp.concatenate([q_ref[:, (hk * SWA_GROUP + g) * HEAD_DIM:(hk * SWA_GROUP + g + 1) * HEAD_DIM]
                             for g in range(SWA_GROUP)], axis=0)
        kb = jnp.concatenate([kp_ref[:, ks], kc_ref[:, ks], kn_ref[:, ks], kx_ref[:, ks]], axis=0)
        s = _dot_nt(q, kb)
        scores.append((s.reshape(SWA_GROUP, SWA_BLOCK, nk) + mask[None]).reshape(SWA_GROUP * SWA_BLOCK, nk))
        sinks.append(jnp.concatenate([jnp.full((SWA_BLOCK, 1), sink_ref[hk * SWA_GROUP + g], F32)
                                      for g in range(SWA_GROUP)], axis=0))
    probs, maxes = [], []
    for hk in range(kv_heads):
        m = jnp.maximum(jnp.max(scores[hk], axis=-1, keepdims=True), sinks[hk])
        probs.append(jnp.exp(scores[hk] - m).astype(BF16))
        maxes.append(m)
    for hk in range(kv_heads):
        ks = slice(hk * HEAD_DIM, (hk + 1) * HEAD_DIM)
        vb = jnp.concatenate([vp_ref[:, ks], vc_ref[:, ks], vn_ref[:, ks], vx_ref[:, ks]], axis=0)
        vb = jnp.concatenate([vb, jnp.ones((nk, HEAD_DIM), BF16)], axis=1)
        o = jnp.dot(probs[hk], vb, preferred_element_type=F32)
        denom = o[:, HEAD_DIM:HEAD_DIM + 1] + jnp.exp(sinks[hk] - maxes[hk])
        o = o[:, :HEAD_DIM] / denom
        for g in range(SWA_GROUP):
            h = hk * SWA_GROUP + g
            acc_ref[:, h * HEAD_DIM:(h + 1) * HEAD_DIM] = o[g * SWA_BLOCK:(g + 1) * SWA_BLOCK]
    o_ref[...] = _rms_gain(acc_ref[...], g_ref[...])


def _swa(dm, u_rot, u_rest, sink, mask_tab, gain):
    nb = dm.S // SWA_BLOCK
    lat_blocks = dm.B * nb
    ctx_per = dm.L // SWA_BLOCK
    n_blocks = lat_blocks + dm.B * ctx_per
    kvw = dm.swa_kv
    k_col = dm.off_swa_k // kvw
    v_col = dm.rest_swa_v // kvw
    ctx_row0 = dm.n_lat // dm.L

    def batch_of(g):
        return jnp.where(g < lat_blocks, g // nb, (g - lat_blocks) // ctx_per)

    def nbr(g, d):
        n = g % nb
        return jnp.where(g < lat_blocks, (g // nb) * nb + jnp.clip(n + d, 0, nb - 1), g)

    def kind(g):
        n = g % nb
        return jnp.where(g < lat_blocks, jnp.where(n == 0, 0, jnp.where(n == nb - 1, 2, 1)), 3)

    blk = lambda col, d: pl.BlockSpec((SWA_BLOCK, kvw), lambda g: (nbr(g, d), col))
    ctx = lambda col: pl.BlockSpec((dm.L, kvw), lambda g: (ctx_row0 + batch_of(g), col))
    in_specs = [pl.BlockSpec(memory_space=pltpu.SMEM),
                pl.BlockSpec((SWA_BLOCK, dm.swa_q), lambda g: (g, 0)),
                blk(k_col, -1), blk(k_col, 0), blk(k_col, 1),
                blk(v_col, -1), blk(v_col, 0), blk(v_col, 1),
                ctx(k_col), ctx(v_col),
                pl.BlockSpec((1, SWA_BLOCK, 3 * SWA_BLOCK), lambda g: (kind(g), 0, 0)),
                pl.BlockSpec((1, dm.swa_q), lambda g: (0, 0))]
    return pl.pallas_call(
        functools.partial(_swa_kernel, kv_heads=dm.swa_kv_heads),
        grid=(n_blocks,),
        in_specs=in_specs,
        out_specs=pl.BlockSpec((SWA_BLOCK, dm.swa_q), lambda g: (g, 0)),
        out_shape=jax.ShapeDtypeStruct((dm.rows, dm.swa_q), BF16),
        scratch_shapes=[pltpu.VMEM((SWA_BLOCK, dm.swa_q), F32)],
        compiler_params=_params(1),
        name="window_attn",
    )(sink, u_rot, u_rot, u_rot, u_rot, u_rest, u_rest, u_rest, u_rot, u_rest, mask_tab, gain)


def _nat_bias_tables(dm, rpb_all):
    nq, nk = NAT_QROWS, 3 * NAT_QROWS
    i = np.arange(nq)[:, None]
    j = np.arange(nk)[None, :]
    row_ok = [(j >= nq) & (j < nq + NAT_ROWS) & (i >= 0),
              (j - i >= 0) & (j - i < NAT_ROWS),
              (j >= 0) & (j < NAT_ROWS) & (i >= 0),
              np.zeros((nq, nk), bool)]
    dr = np.clip(j - i + (NAT_ROWS - 1) - nq, 0, 2 * NAT_ROWS - 2)
    t_row = np.zeros((2 * NAT_ROWS - 1, nq, nk), np.float32)
    for a in range(nq):
        for b in range(nk):
            t_row[dr[a, b], a, b] = 1.0
    cq = np.arange(GRID_W)[:, None]
    kc = np.arange(GRID_W)[None, :]
    cs = np.clip(cq - NAT_COLS // 2, 0, GRID_W - NAT_COLS)
    col_ok = (kc >= cs) & (kc < cs + NAT_COLS)
    dc = np.clip(kc - cq + NAT_COLS - 1, 0, 2 * NAT_COLS - 2)
    t_col = np.zeros((2 * NAT_COLS - 1, GRID_W, GRID_W), np.float32)
    for a in range(GRID_W):
        for b in range(GRID_W):
            t_col[dc[a, b], a, b] = 1.0
    val = jnp.einsum('lhrc,rij,cqk->lhiqjk', rpb_all, jnp.asarray(t_row), jnp.asarray(t_col),
                     precision=lax.Precision.HIGHEST)
    ok = jnp.asarray(np.broadcast_to(col_ok[None, :, None, :], (nq, GRID_W, nk, GRID_W)))
    bias = jnp.where(ok[None, None], val.astype(BF16), jnp.asarray(NEG_INF, BF16))
    depth, H = bias.shape[:2]
    bias = bias.reshape(depth, H, NAT_QBLOCK, 3 * NAT_QBLOCK)
    kmask = np.stack([np.broadcast_to(r[:, None, :, None], (nq, GRID_W, nk, GRID_W)).reshape(NAT_QBLOCK, 3 * NAT_QBLOCK)
                      for r in row_ok])
    kmask = jnp.asarray(np.where(kmask, 0.0, NEG_INF).astype(np.float32)).astype(BF16)
    return bias, kmask


def _nat_kernel(q_ref, kp_ref, kc_ref, kn_ref, vp_ref, vc_ref, vn_ref, kx_ref, vx_ref,
                bias_ref, kmask_ref, g_ref, o_ref, acc_ref, *, heads):
    n_ctx = kx_ref.shape[0]
    no_bias = jnp.zeros((NAT_QBLOCK, n_ctx), F32)
    kmask = kmask_ref[0].astype(F32)
    nk = 3 * NAT_QBLOCK + n_ctx
    for h in range(heads):
        hs = slice(h * HEAD_DIM, (h + 1) * HEAD_DIM)
        q = q_ref[:, hs]
        kb = jnp.concatenate([kp_ref[:, hs], kc_ref[:, hs], kn_ref[:, hs], kx_ref[:, hs]], axis=0)
        vb = jnp.concatenate([vp_ref[:, hs], vc_ref[:, hs], vn_ref[:, hs], vx_ref[:, hs]], axis=0)
        vb = jnp.concatenate([vb, jnp.ones((nk, HEAD_DIM), BF16)], axis=1)
        s = _dot_nt(q, kb) + jnp.concatenate([bias_ref[h].astype(F32) + kmask, no_bias], axis=1)
        m = jnp.max(s, axis=-1, keepdims=True)
        p = jnp.exp(s - m)
        o = jnp.dot(p.astype(BF16), vb, preferred_element_type=F32)
        acc_ref[:, hs] = o[:, :HEAD_DIM] / o[:, HEAD_DIM:HEAD_DIM + 1]
    o_ref[...] = _rms_gain(acc_ref[...], g_ref[...])


def _nat(dm, u, bias_tab, kmask, gain):
    nb = dm.S // NAT_QBLOCK
    lat_blocks = dm.B * nb
    ctx_per = dm.L // NAT_QBLOCK
    n_blocks = lat_blocks + dm.B * ctx_per
    w = dm.nat_w
    q_col, k_col, v_col = dm.rest_nat_q // w, dm.rest_nat_k // w, dm.rest_nat_v // w
    ctx_row0 = dm.n_lat // dm.L

    def batch_of(g):
        return jnp.where(g < lat_blocks, g // nb, (g - lat_blocks) // ctx_per)

    def nbr(g, d):
        n = g % nb
        return jnp.where(g < lat_blocks, (g // nb) * nb + jnp.clip(n + d, 0, nb - 1), g)

    def kind(g):
        n = g % nb
        return jnp.where(g < lat_blocks, jnp.where(n == 0, 0, jnp.where(n == nb - 1, 2, 1)), 3)

    blk = lambda col, d: pl.BlockSpec((NAT_QBLOCK, w), lambda g: (nbr(g, d), col))
    ctx = lambda col: pl.BlockSpec((dm.L, w), lambda g: (ctx_row0 + batch_of(g), col))
    in_specs = [pl.BlockSpec((NAT_QBLOCK, w), lambda g: (g, q_col)),
                blk(k_col, -1), blk(k_col, 0), blk(k_col, 1),
                blk(v_col, -1), blk(v_col, 0), blk(v_col, 1),
                ctx(k_col), ctx(v_col),
                pl.BlockSpec((dm.nat_heads, NAT_QBLOCK, 3 * NAT_QBLOCK), lambda g: (0, 0, 0)),
                pl.BlockSpec((1, NAT_QBLOCK, 3 * NAT_QBLOCK), lambda g: (kind(g), 0, 0)),
                pl.BlockSpec((1, w), lambda g: (0, 0))]
    return pl.pallas_call(
        functools.partial(_nat_kernel, heads=dm.nat_heads),
        grid=(n_blocks,),
        in_specs=in_specs,
        out_specs=pl.BlockSpec((NAT_QBLOCK, w), lambda g: (g, 0)),
        out_shape=jax.ShapeDtypeStruct((dm.rows, w), BF16),
        scratch_shapes=[pltpu.VMEM((NAT_QBLOCK, w), F32)],
        compiler_params=_params(1),
        name="nbr_attn",
    )(u, u, u, u, u, u, u, u, u, bias_tab, kmask, gain)


CONV_TILE = 256
HALO = 16


def _conv_kernel(u_ref, prev_ref, next_ref, w_ref, b_ref, g_ref, o_ref, *, cw, tiles_per_seq, lat_tiles):
    i = pl.program_id(0)
    is_ctx = i >= lat_tiles
    n = i % tiles_per_seq
    has_prev = jnp.logical_and(jnp.logical_not(is_ctx), n != 0)
    has_next = jnp.logical_and(jnp.logical_not(is_ctx), n != tiles_per_seq - 1)
    x = u_ref[:, 0:cw].astype(F32)
    bg = u_ref[:, cw:2 * cw].astype(F32)
    cg = u_ref[:, 2 * cw:3 * cw].astype(F32)
    z = cg * x
    zp = (prev_ref[HALO - 1:HALO, 2 * cw:3 * cw].astype(F32) * prev_ref[HALO - 1:HALO, 0:cw].astype(F32))
    zn = (next_ref[0:1, 2 * cw:3 * cw].astype(F32) * next_ref[0:1, 0:cw].astype(F32))
    zp = jnp.where(has_prev, zp, 0.0)
    zn = jnp.where(has_next, zn, 0.0)
    row = lax.broadcasted_iota(jnp.int32, z.shape, 0)
    z_m1 = jnp.where(row == 0, zp, pltpu.roll(z, 1, 0))
    z_p1 = jnp.where(row == CONV_TILE - 1, zn, pltpu.roll(z, CONV_TILE - 1, 0))
    w = w_ref[...]
    conv = b_ref[...] + z_m1 * w[0:1] + z * w[1:2] + z_p1 * w[2:3]
    o_ref[...] = _rms_gain(bg * conv, g_ref[...])


def _conv(dm, u, w, b, gain):
    assert dm.L == CONV_TILE, "context sequences are one convolution tile"
    cw = dm.conv_w
    tiles = dm.rows // CONV_TILE
    lat_tiles = dm.n_lat // CONV_TILE
    col = 0
    per_halo = CONV_TILE // HALO
    last_halo = dm.rows // HALO - 1
    kern = functools.partial(_conv_kernel, cw=cw, tiles_per_seq=dm.S // CONV_TILE, lat_tiles=lat_tiles)
    return pl.pallas_call(
        kern,
        grid=(tiles,),
        in_specs=[pl.BlockSpec((CONV_TILE, 3 * cw), lambda i: (i, col)),
                  pl.BlockSpec((HALO, 3 * cw), lambda i: (jnp.maximum(i * per_halo - 1, 0), col)),
                  pl.BlockSpec((HALO, 3 * cw), lambda i: (jnp.minimum((i + 1) * per_halo, last_halo), col)),
                  pl.BlockSpec((3, cw), lambda i: (0, 0)),
                  pl.BlockSpec((1, cw), lambda i: (0, 0)),
                  pl.BlockSpec((1, cw), lambda i: (0, 0))],
        out_specs=pl.BlockSpec((CONV_TILE, cw), lambda i: (i, 0)),
        out_shape=jax.ShapeDtypeStruct((dm.rows, cw), BF16),
        compiler_params=_params(1),
        name="gated_conv",
    )(u, u, u, w, b, gain)


def _layer_norm(y, g, b):
    mu = jnp.mean(y, axis=-1, keepdims=True)
    yc = y - mu
    var = jnp.mean(yc * yc, axis=-1, keepdims=True)
    return yc * lax.rsqrt(var + LN_EPS) * g + b


def _route(logits):
    lane = lax.broadcasted_iota(jnp.int32, logits.shape, 1).astype(F32)
    big = float(ROUTER_LANES)
    is_g = lane < N_GROUPS
    gl = jnp.where(is_g, logits, NEG_INF)
    gmax = jnp.max(gl, axis=-1, keepdims=True)
    ge = jnp.where(is_g, jnp.exp(gl - gmax), 0.0)
    gprob = ge / jnp.sum(ge, axis=-1, keepdims=True)
    g_p = jnp.max(gprob, axis=-1, keepdims=True)
    g_idx = jnp.min(jnp.where(jnp.logical_and(is_g, gprob == g_p), lane, big), axis=-1, keepdims=True)
    lo = N_GROUPS + g_idx * EXPERTS_PER_GROUP
    in_grp = jnp.logical_and(lane >= lo, lane < lo + EXPERTS_PER_GROUP)
    el = jnp.where(in_grp, logits, NEG_INF)
    v1 = jnp.max(el, axis=-1, keepdims=True)
    i1 = jnp.min(jnp.where(jnp.logical_and(in_grp, el == v1), lane, big), axis=-1, keepdims=True)
    el2 = jnp.where(lane == i1, NEG_INF, el)
    v2 = jnp.max(el2, axis=-1, keepdims=True)
    rest = jnp.logical_and(in_grp, lane != i1)
    i2 = jnp.min(jnp.where(jnp.logical_and(rest, el2 == v2), lane, big), axis=-1, keepdims=True)
    e2 = jnp.exp(v2 - v1)
    den = 1.0 + e2
    w1 = g_p / den
    w2 = g_p * (e2 / den)
    gates = jnp.where(lane == i1, w1, jnp.where(lane == i2, w2, 0.0))
    return jnp.where(lane == ROUTE_GROUP_LANE, g_idx, gates)


HIGH_HALF = 0xFFFF0000


def _pack_bf16_pairs(x):
    w = x.shape[1] // 2
    hi = lax.bitcast_convert_type(x[:, :w].astype(BF16).astype(F32), jnp.uint32)
    lo = lax.bitcast_convert_type(x[:, w:].astype(BF16).astype(F32), jnp.uint32)
    return hi | (lo >> 16)


def _unpack_bf16_pairs(p):
    hi = lax.bitcast_convert_type(p & jnp.uint32(HIGH_HALF), F32)
    lo = lax.bitcast_convert_type(p << 16, F32)
    return hi, lo


def _start_row_gather(idx_ref, idx0, src_hbm, dst_ref, sem, row0, n_rows):
    def issue(k, carry):
        j = row0 + k
        row = idx_ref[idx0 + j]
        pltpu.make_async_copy(src_hbm.at[pl.ds(row, 1)], dst_ref.at[pl.ds(j, 1)], sem).start()
        return carry

    lax.fori_loop(0, n_rows, issue, 0, unroll=8)


def _wait_row_gather(src_hbm, dst_ref, sem):
    pltpu.make_async_copy(src_hbm.at[pl.ds(0, dst_ref.shape[0])], dst_ref, sem).wait()


def _ln1_kernel(h_ref, mix_ref, mod_ref, g_ref, b_ref, wr_ref, wl_ref, br_ref, h_out, t_out, route_out, *,
                D, alpha):
    mod = mod_ref[0]
    y = alpha * h_ref[...] + mod[:, 2 * D:3 * D] * mix_ref[...].astype(F32)
    hn = _layer_norm(y, g_ref[...], b_ref[...])
    h_out[...] = hn
    t = hn * (1.0 + mod[:, 4 * D:5 * D]) + mod[:, 3 * D:4 * D]
    t_hi = t.astype(BF16)
    t_lo = (t - t_hi.astype(F32)).astype(BF16)
    logits = (jnp.dot(t_hi, wr_ref[...], preferred_element_type=F32)
              + jnp.dot(t_lo, wr_ref[...], preferred_element_type=F32)
              + jnp.dot(t_hi, wl_ref[...], preferred_element_type=F32)) + br_ref[...]
    route = _route(logits)
    route_out[...] = route
    t_out[:, 0:D // 2] = _pack_bf16_pairs(t)
    t_out[:, D // 2:] = lax.bitcast_convert_type(route, jnp.uint32)


def _ln1(dm, rows, h, mix, mod, g, b, wr_hi, wr_lo, br):
    tr, D = dm.tr, dm.D
    row = lambda width: pl.BlockSpec((tr, width), lambda i: (i, 0))
    vec = lambda width: pl.BlockSpec((1, width), lambda i: (0, 0))
    return pl.pallas_call(
        functools.partial(_ln1_kernel, D=D, alpha=dm.alpha),
        grid=(rows // tr,),
        in_specs=[row(D), row(D), _mod_spec(dm, tr), vec(D), vec(D),
                  pl.BlockSpec((D, ROUTER_LANES), lambda i: (0, 0)),
                  pl.BlockSpec((D, ROUTER_LANES), lambda i: (0, 0)), vec(ROUTER_LANES)],
        out_specs=[row(D), row(dm.packed_w), row(ROUTER_LANES)],
        out_shape=[jax.ShapeDtypeStruct((rows, D), F32), jax.ShapeDtypeStruct((rows, dm.packed_w), jnp.uint32),
                   jax.ShapeDtypeStruct((rows, ROUTER_LANES), F32)],
        compiler_params=_params(1),
        name="mix_residual_norm_route",
    )(h, mix, mod, g, b, wr_hi, wr_lo, br)


def _ln2_kernel(pos_ref, h_ref, f_hbm, mod_ref, modn_ref, g_ref, b_ref, h_out, a_out, fbuf, sem, *,
                D, alpha, tr, n_tiles):
    i = pl.program_id(0)
    slot = i % 2

    @pl.when(i == 0)
    def _():
        _start_row_gather(pos_ref, 0, f_hbm, fbuf.at[0], sem.at[0], 0, tr)

    @pl.when(i + 1 < n_tiles)
    def _():
        _start_row_gather(pos_ref, (i + 1) * tr, f_hbm, fbuf.at[1 - slot], sem.at[1 - slot], 0, tr)

    _wait_row_gather(f_hbm, fbuf.at[slot], sem.at[slot])
    mod = mod_ref[0]
    f_hi, f_lo = _unpack_bf16_pairs(fbuf[slot])
    ffn = jnp.concatenate([f_hi, f_lo], axis=1)
    y = alpha * h_ref[...] + mod[:, 5 * D:6 * D] * ffn
    hn = _layer_norm(y, g_ref[...], b_ref[...])
    h_out[...] = hn
    if a_out is not None:
        modn = modn_ref[0]
        a_out[...] = (hn * (1.0 + modn[:, D:2 * D]) + modn[:, 0:D]).astype(BF16)


def _ln2_last_kernel(pos_ref, h_ref, f_hbm, mod_ref, g_ref, b_ref, h_out, fbuf, sem, **kw):
    _ln2_kernel(pos_ref, h_ref, f_hbm, mod_ref, None, g_ref, b_ref, h_out, None, fbuf, sem, **kw)


def _ln2(dm, rows, pos, h, ffn_sorted, mod, mod_next, g, b):
    tr, D = dm.tr, dm.D
    n_tiles = rows // tr
    row = pl.BlockSpec((tr, D), lambda i, p: (i, 0))
    vec = pl.BlockSpec((1, D), lambda i, p: (0, 0))
    hbm = pl.BlockSpec(memory_space=pl.ANY)
    scratch = [pltpu.VMEM((2, tr, D // 2), jnp.uint32), pltpu.SemaphoreType.DMA((2,))]
    kw = dict(D=D, alpha=dm.alpha, tr=tr, n_tiles=n_tiles)
    if mod_next is None:
        return pl.pallas_call(
            functools.partial(_ln2_last_kernel, **kw),
            grid_spec=pltpu.PrefetchScalarGridSpec(
                num_scalar_prefetch=1, grid=(n_tiles,),
                in_specs=[row, hbm, _mod_spec(dm, tr), vec, vec],
                out_specs=row, scratch_shapes=scratch),
            out_shape=jax.ShapeDtypeStruct((rows, D), F32),
            compiler_params=_params(1),
            name="ffn_residual_norm_last",
        )(pos, h, ffn_sorted, mod, g, b), None
    return pl.pallas_call(
        functools.partial(_ln2_kernel, **kw),
        grid_spec=pltpu.PrefetchScalarGridSpec(
            num_scalar_prefetch=1, grid=(n_tiles,),
            in_specs=[row, hbm, _mod_spec(dm, tr), _mod_spec(dm, tr), vec, vec],
            out_specs=[row, row], scratch_shapes=scratch),
        out_shape=[jax.ShapeDtypeStruct((rows, D), F32), jax.ShapeDtypeStruct((rows, D), BF16)],
        compiler_params=_params(1),
        name="ffn_residual_norm",
    )(pos, h, ffn_sorted, mod, mod_next, g, b)


def _dispatch_plan(rows, route):
    tm = EXPERT_TILE
    n_tiles = rows // tm + N_GROUPS
    i32 = jnp.int32
    gidx = route[:, ROUTE_GROUP_LANE].astype(i32)
    onehot = (gidx[:, None] == jnp.arange(N_GROUPS, dtype=i32)[None]).astype(i32)
    csum = jnp.cumsum(onehot, axis=0)
    counts = csum[-1]
    rank = jnp.sum(onehot * csum, axis=1) - 1
    padded = (counts + tm - 1) // tm * tm
    pstart = jnp.cumsum(padded) - padded
    ustart = jnp.cumsum(counts) - counts
    total = jnp.sum(padded)
    pos = jnp.sum(onehot * pstart[None], axis=1) + rank
    order = jnp.argsort(gidx, stable=True).astype(i32)
    slot = jnp.arange(n_tiles * tm, dtype=i32)
    sgrp = jnp.sum((slot[:, None] >= pstart[None, 1:]).astype(i32), axis=1)
    shot = (sgrp[:, None] == jnp.arange(N_GROUPS, dtype=i32)[None]).astype(i32)
    k = slot - jnp.sum(shot * pstart[None], axis=1)
    valid = jnp.logical_and(k < jnp.sum(shot * counts[None], axis=1), slot < total)
    sorted_at = jnp.clip(jnp.sum(shot * ustart[None], axis=1) + k, 0, rows - 1)
    src = jnp.where(valid, order[sorted_at], slot % rows)
    tile_slot = jnp.arange(n_tiles, dtype=i32) * tm
    tgrp = jnp.where(tile_slot < total, sgrp[::tm], N_GROUPS)
    return src, pos.astype(i32), tgrp.astype(i32)


def _experts_kernel(src_ref, tgrp_ref, t_hbm, wg_ref, wu_ref, wd_ref, o_ref, tbuf, sem, tb_ref, gate_ref, acc_ref,
                    *, D, tm, n_tiles):
    i = pl.program_id(0)
    e = pl.program_id(1)
    slot = i % 2
    grp = tgrp_ref[i]
    active = grp < N_GROUPS
    half = D // 2
    part_rows = tm // EXPERT_STEPS
    F = wg_ref.shape[-1]

    @pl.when(jnp.logical_and(i == 0, e == 0))
    def _():
        _start_row_gather(src_ref, 0, t_hbm, tbuf.at[0], sem.at[0], 0, tm)

    @pl.when(e == 0)
    def _():
        _wait_row_gather(t_hbm, tbuf.at[slot], sem.at[slot])
        hi, lo = _unpack_bf16_pairs(tbuf[slot, :, 0:half])
        tb_ref[:, 0:half] = hi.astype(BF16)
        tb_ref[:, half:] = lo.astype(BF16)
        gate_ref[...] = lax.bitcast_convert_type(tbuf[slot, :, half:], F32)

    has_next = i + 1 < n_tiles

    @pl.when(jnp.logical_and(has_next, jnp.logical_not(active)))
    def _():
        _start_row_gather(src_ref, (i + 1) * tm, t_hbm, tbuf.at[1 - slot], sem.at[1 - slot],
                          e * part_rows, part_rows)

    @pl.when(active)
    def _():
        nxt = jnp.minimum(i + 1, n_tiles - 1) * tm + e * part_rows
        for k in range(part_rows):
            pltpu.make_async_copy(t_hbm.at[pl.ds(src_ref[nxt + k], 1)],
                                  tbuf.at[1 - slot, pl.ds(e * part_rows + k, 1)], sem.at[1 - slot]).start()
        t = tb_ref[...]
        gates = gate_ref[...]
        lane = lax.broadcasted_iota(jnp.int32, gates.shape, 1)
        hids = []
        for k in range(EXPERTS_PER_STEP):
            a = jnp.dot(t, wg_ref[k], preferred_element_type=F32)
            b = jnp.dot(t, wu_ref[k], preferred_element_type=F32)
            expert_lane = N_GROUPS + grp * EXPERTS_PER_GROUP + e * EXPERTS_PER_STEP + k
            gcol = jnp.sum(jnp.where(lane == expert_lane, gates, 0.0), axis=-1, keepdims=True)
            hids.append((a * jax.nn.sigmoid(a) * b * gcol).astype(BF16))
        hid = jnp.concatenate(hids, axis=1)
        part = jnp.dot(hid, wd_ref[...].reshape(EXPERTS_PER_STEP * F, D), preferred_element_type=F32)

        @pl.when(e == 0)
        def _():
            acc_ref[...] = part

        @pl.when(e == EXPERT_STEPS - 1)
        def _():
            o_ref[...] = _pack_bf16_pairs(acc_ref[...] + part)

    @pl.when(jnp.logical_and(e == EXPERT_STEPS - 1, jnp.logical_not(active)))
    def _():
        o_ref[...] = jnp.zeros(o_ref.shape, o_ref.dtype)

    @pl.when(jnp.logical_and(active, jnp.logical_and(jnp.logical_not(has_next), e == EXPERT_STEPS - 1)))
    def _():
        _wait_row_gather(t_hbm, tbuf.at[1 - slot], sem.at[1 - slot])


def _experts(dm, layer, src, tgrp, t_packed, wg_all, wu_all, wd_all):
    tm, D, F = EXPERT_TILE, dm.D, dm.d_expert
    n_tiles = tgrp.shape[0]

    def w_map(i, e, sr, tg):
        return (layer, jnp.minimum(tg[i], N_GROUPS - 1) * EXPERT_STEPS + e, 0, 0)

    return pl.pallas_call(
        functools.partial(_experts_kernel, D=D, tm=tm, n_tiles=n_tiles),
        grid_spec=pltpu.PrefetchScalarGridSpec(
            num_scalar_prefetch=2,
            grid=(n_tiles, EXPERT_STEPS),
            in_specs=[pl.BlockSpec(memory_space=pl.ANY),
                      pl.BlockSpec((None, EXPERTS_PER_STEP, D, F), w_map),
                      pl.BlockSpec((None, EXPERTS_PER_STEP, D, F), w_map),
                      pl.BlockSpec((None, EXPERTS_PER_STEP, F, D), w_map)],
            out_specs=pl.BlockSpec((tm, D // 2), lambda i, e, sr, tg: (i, 0)),
            scratch_shapes=[pltpu.VMEM((2, tm, dm.packed_w), jnp.uint32), pltpu.SemaphoreType.DMA((2,)),
                            pltpu.VMEM((tm, D), BF16), pltpu.VMEM((tm, ROUTER_LANES), F32),
                            pltpu.VMEM((tm, D), F32)]),
        out_shape=jax.ShapeDtypeStruct((n_tiles * tm, D // 2), jnp.uint32),
        compiler_params=pltpu.CompilerParams(dimension_semantics=("arbitrary", "arbitrary"),
                                             vmem_limit_bytes=EXPERTS_VMEM_LIMIT),
        name="experts",
    )(src, tgrp, t_packed, wg_all, wu_all, wd_all)


def kernel(x, c, ctx, c_ctx, w_ada, b_ada, w_in, conv_w, conv_b, attn_sink, nat_rpb, mix_norm_g, w_out,
           ln1_g, ln1_b, w_router_group, b_router_group, w_router_expert, b_router_expert,
           w_gate, w_up, w_down, ln2_g, ln2_b):
    B, S, D = x.shape
    L = ctx.shape[1]
    depth = w_in.shape[0]
    dm = Dims(B, S, L, D, depth, w_gate.shape[-1])

    xin = jnp.concatenate([c, c_ctx[None], jnp.zeros((MOD_ROWS - B - 1, D), F32)], axis=0)
    mods = _ada_all(xin, w_ada, b_ada).reshape(depth, MOD_ROWS, 1, 6 * D)

    cos_t, sin_t = _rope_tables(dm)
    swa_mask = _swa_mask_table()
    nat_bias, nat_kmask = _nat_bias_tables(dm, nat_rpb)

    w_gate_b = w_gate.astype(BF16)
    w_up_b = w_up.astype(BF16)
    w_down_b = w_down.astype(BF16)
    pad = jnp.zeros((depth, D, ROUTER_LANES - N_GROUPS - N_EXPERTS), F32)
    w_route = jnp.concatenate([w_router_group, w_router_expert, pad], axis=-1)
    w_route_hi = w_route.astype(BF16)
    w_route_lo = (w_route - w_route_hi.astype(F32)).astype(BF16)
    b_route = jnp.concatenate([b_router_group, b_router_expert, pad[:, 0]], axis=-1).reshape(depth, 1, ROUTER_LANES)

    h, a = _modulate(dm, x.reshape(B * S, D), ctx.reshape(B * L, D), mods[0])
    for i in range(depth):
        last = i == depth - 1
        rows = dm.n_lat if last else dm.rows
        u_rot, u_rest = _inproj(dm, a, w_in, i, cos_t, sin_t)
        gain = mix_norm_g[i].reshape(1, D)
        y_swa = _swa(dm, u_rot, u_rest, attn_sink[i], swa_mask, gain[:, :dm.swa_q])
        y_conv = _conv(dm, u_rest, conv_w[i], conv_b[i].reshape(1, -1), gain[:, dm.swa_q:dm.swa_q + dm.conv_w])
        y_nat = _nat(dm, u_rest, nat_bias[i], nat_kmask, gain[:, dm.swa_q + dm.conv_w:])
        mix = _matmul([y_swa, y_conv, y_nat], w_out, i, rows, dm.tm, 512, BF16, "out_proj")
        h_mid, t_packed, route = _ln1(dm, rows, h, mix, mods[i], ln1_g[i].reshape(1, D), ln1_b[i].reshape(1, D),
                                      w_route_hi[i], w_route_lo[i], b_route[i])
        src, pos, tgrp = _dispatch_plan(rows, route)
        ffn_sorted = _experts(dm, i, src, tgrp, t_packed, w_gate_b, w_up_b, w_down_b)
        h, a = _ln2(dm, rows, pos, h_mid, ffn_sorted, mods[i], None if last else mods[i + 1],
                    ln2_g[i].reshape(1, D), ln2_b[i].reshape(1, D))
    return h.reshape(B, S, D)
```
